```python
import math
import jax, jax.numpy as jnp
from jax import lax
import numpy as np

D_MODEL = 2048
BATCH = 1
SEQ = 16384
DEPTH = 2

N_MEM = 256
N_MIXERS = 2
HEAD_DIM = 128
EPS = 1e-6
X_HEADS = 4
X_WIDTH = X_HEADS * HEAD_DIM
BRANCH_WIDTH = D_MODEL - X_WIDTH
MIX_WIDTH = BRANCH_WIDTH + X_WIDTH
GLA_HEADS = 4
GLA_DV = BRANCH_WIDTH // GLA_HEADS
GLA_DK = GLA_DV // 2
GLA_KEY_WIDTH = GLA_HEADS * GLA_DK
GLA_GATE_RANK = 16
GLA_GATE_TEMP = 16.0
GLA_CHUNK = 64
NSA_HEADS = BRANCH_WIDTH // HEAD_DIM
NSA_GROUPS = 2
NSA_HPG = NSA_HEADS // NSA_GROUPS
NSA_KV_WIDTH = NSA_GROUPS * HEAD_DIM
N_BRANCH = 3
CMP_LEN = 32
CMP_STRIDE = 16
CMP_HIDDEN = 256
SLC_LEN = 64
SLC_TOPN = 16
WINDOW = 512
Q_BLOCK = 128
FORCED = 1e4

GLA_SIZES = [GLA_KEY_WIDTH, GLA_KEY_WIDTH, BRANCH_WIDTH, GLA_GATE_RANK, BRANCH_WIDTH, X_WIDTH, X_WIDTH]
NSA_SIZES = [BRANCH_WIDTH] + [NSA_KV_WIDTH] * 6 + [NSA_HEADS * N_BRANCH, BRANCH_WIDTH, X_WIDTH, X_WIDTH]
GLA_IN = sum(GLA_SIZES)
NSA_IN = sum(NSA_SIZES)
N_GLA_LAYERS = (DEPTH + 1) // 2
N_NSA_LAYERS = DEPTH // 2

kernel_name = 'hybrid_gla_nsa_memxattn'


def rmsnorm(x, g):
    xf = x.astype(jnp.float32)
    y = xf * lax.rsqrt(jnp.mean(xf * xf, axis=-1, keepdims=True) + EPS)
    return (y * g.astype(jnp.float32)).astype(x.dtype)


def split_cols(a, sizes):
    return jnp.split(a, np.cumsum(sizes)[:-1].tolist(), axis=-1)


def masked_softmax(s, mask):
    p = jax.nn.softmax(jnp.where(mask, s, -1e30), axis=-1)
    return jnp.where(mask, p, 0.0)


def mem_cross_attention(xq, mem_n, w_kv):
    B, T, _ = xq.shape
    M = mem_n.shape[1]
    k, v = jnp.split(mem_n @ w_kv, 2, axis=-1)
    q = xq.reshape(B, T, X_HEADS, HEAD_DIM)
    k = k.reshape(B, M, X_HEADS, HEAD_DIM)
    v = v.reshape(B, M, X_HEADS, HEAD_DIM)
    s = jnp.einsum('bthd,bmhd->bhtm', q, k).astype(jnp.float32) * HEAD_DIM ** -0.5
    p = jax.nn.softmax(s, axis=-1).astype(v.dtype)
    return jnp.einsum('bhtm,bmhd->bthd', p, v).reshape(B, T, X_WIDTH)


def gla_chunk_scan(q, k, v, logg):
    B, T, H, DK = q.shape
    DV = v.shape[-1]
    C = GLA_CHUNK
    N = T // C

    def to_chunks(a):
        return a.reshape(B, N, C, H, a.shape[-1]).transpose(1, 0, 3, 2, 4)

    qc, kc, vc, gc = to_chunks(q), to_chunks(k), to_chunks(v), to_chunks(logg)
    causal = jnp.tril(jnp.ones((C, C), dtype=bool))[:, :, None]

    def step(S, inp):
        qi, ki, vi, gi = inp
        b = jnp.cumsum(gi, axis=2)
        b_last = b[:, :, -1:, :]
        diff = b[:, :, :, None, :] - b[:, :, None, :, :]
        decay = jnp.exp(jnp.where(causal, diff, -jnp.inf))
        attn = jnp.einsum('bhcd,bhsd,bhcsd->bhcs', qi, ki, decay)
        o = jnp.einsum('bhcd,bhde->bhce', qi * jnp.exp(b), S) + jnp.einsum('bhcs,bhse->bhce', attn, vi)
        S = jnp.exp(b_last[:, :, 0, :])[..., None] * S + jnp.einsum('bhsd,bhse->bhde', ki * jnp.exp(b_last - b), vi)
        return S, o

    S0 = jnp.zeros((B, H, DK, DV), jnp.float32)
    _, o = lax.scan(step, S0, (qc, kc, vc, gc))
    return o.transpose(1, 0, 3, 2, 4).reshape(B, T, H, DV)


def gla_mixer(h, mem_n, w_in, w_gate_lr, b_gate, g_norm, w_kv, w_out):
    B, T, _ = h.shape
    f32 = jnp.float32
    q, k, v, glr, r, xq, xg = split_cols(h @ w_in, GLA_SIZES)
    q = q.reshape(B, T, GLA_HEADS, GLA_DK).astype(f32) * GLA_DK ** -0.5
    k = k.reshape(B, T, GLA_HEADS, GLA_DK).astype(f32)
    v = v.reshape(B, T, GLA_HEADS, GLA_DV).astype(f32)
    logg = jax.nn.log_sigmoid((glr @ w_gate_lr + b_gate).astype(f32)) / GLA_GATE_TEMP
    logg = logg.reshape(B, T, GLA_HEADS, GLA_DK)
    o = gla_chunk_scan(q, k, v, logg)
    o = rmsnorm(o, g_norm).reshape(B, T, BRANCH_WIDTH).astype(h.dtype)
    y_seq = o * jax.nn.silu(r)
    y_mem = mem_cross_attention(xq, mem_n, w_kv) * jax.nn.silu(xg)
    return jnp.concatenate([y_seq, y_mem], axis=-1) @ w_out


def nsa_mixer(h, mem_n, w_in, pe_k, pe_v, wk1, wk2, wv1, wv2, w_kv, w_out):
    B, T, _ = h.shape
    dt = h.dtype
    f32 = jnp.float32
    G, HPG, HD = NSA_GROUPS, NSA_HPG, HEAD_DIM
    scale = HD ** -0.5
    q, kc, vc, ks, vs, kw, vw, gl, r, xq, xg = split_cols(h @ w_in, NSA_SIZES)
    q = q.reshape(B, T, G, HPG, HD)
    gates = jax.nn.sigmoid(gl.astype(f32)).reshape(B, T, G, HPG, N_BRANCH)
    kc, vc, ks, vs, kw, vw = [a.reshape(B, T, G, HD) for a in (kc, vc, ks, vs, kw, vw)]

    n_cmp = (T - CMP_LEN) // CMP_STRIDE + 1
    cidx = np.arange(n_cmp)[:, None] * CMP_STRIDE + np.arange(CMP_LEN)[None, :]

    def compress(a, pe, w1, w2):
        blocks = a[:, cidx] + pe[None, None, :, None, :]
        blocks = blocks.transpose(0, 1, 3, 2, 4).reshape(B, n_cmp, G, CMP_LEN * HD)
        return jax.nn.silu(blocks @ w1) @ w2

    k_cmp = compress(kc, pe_k, wk1, wk2)
    v_cmp = compress(vc, pe_v, wv1, wv2)
    cmp_end = jnp.asarray(cidx[:, -1])

    n_slc = T // SLC_LEN
    top_n = min(SLC_TOPN, n_slc)
    cstart = np.arange(n_cmp) * CMP_STRIDE
    sstart = np.arange(n_slc) * SLC_LEN
    overlap = jnp.asarray(((cstart[:, None] < sstart[None, :] + SLC_LEN) &
                           (cstart[:, None] + CMP_LEN > sstart[None, :])).astype(np.float32))
    k_sb = ks.reshape(B, n_slc, SLC_LEN, G, HD).transpose(0, 3, 1, 2, 4)
    v_sb = vs.reshape(B, n_slc, SLC_LEN, G, HD).transpose(0, 3, 1, 2, 4)

    kw_pad = jnp.pad(kw, ((0, 0), (WINDOW, 0), (0, 0), (0, 0)))
    vw_pad = jnp.pad(vw, ((0, 0), (WINDOW, 0), (0, 0), (0, 0)))

    n_qb = T // Q_BLOCK
    q_blocks = q.reshape(B, n_qb, Q_BLOCK, G, HPG, HD).transpose(1, 0, 2, 3, 4, 5)
    g_blocks = gates.reshape(B, n_qb, Q_BLOCK, G, HPG, N_BRANCH).transpose(1, 0, 2, 3, 4, 5)
    bi = jnp.arange(B)[:, None, None, None]
    gi = jnp.arange(G)[None, :, None, None]
    jblk = jnp.arange(n_slc)

    def block_fn(args):
        qb_idx, qb, gb = args
        start = qb_idx * Q_BLOCK
        t = start + jnp.arange(Q_BLOCK)
        s = jnp.einsum('bqghd,bngd->bghqn', qb, k_cmp).astype(f32) * scale
        p_cmp = masked_softmax(s, cmp_end[None, :] <= t[:, None])
        o_cmp = jnp.einsum('bghqn,bngd->bqghd', p_cmp.astype(dt), v_cmp)
        imp = jnp.einsum('bghqn,nj->bgqj', p_cmp, overlap)
        cur = t // SLC_LEN
        forced = (jblk[None, :] == 0) | (jblk[None, :] == cur[:, None]) | (jblk[None, :] == cur[:, None] - 1)
        imp = jnp.where(forced, FORCED, imp)
        imp = jnp.where(jblk[None, :] > cur[:, None], -FORCED, imp)
        _, sel = lax.top_k(imp, top_n)
        ksel = k_sb[bi, gi, sel].reshape(B, G, Q_BLOCK, top_n * SLC_LEN, HD)
        vsel = v_sb[bi, gi, sel].reshape(B, G, Q_BLOCK, top_n * SLC_LEN, HD)
        pos = (sel[..., None] * SLC_LEN + jnp.arange(SLC_LEN)).reshape(B, G, Q_BLOCK, top_n * SLC_LEN)
        s = jnp.einsum('bqghd,bgqkd->bghqk', qb, ksel).astype(f32) * scale
        p = masked_softmax(s, (pos <= t[None, None, :, None])[:, :, None])
        o_slc = jnp.einsum('bghqk,bgqkd->bqghd', p.astype(dt), vsel)
        kwin = lax.dynamic_slice_in_dim(kw_pad, start, Q_BLOCK + WINDOW, axis=1)
        vwin = lax.dynamic_slice_in_dim(vw_pad, start, Q_BLOCK + WINDOW, axis=1)
        wpos = start - WINDOW + jnp.arange(Q_BLOCK + WINDOW)
        m_win = (wpos[None, :] <= t[:, None]) & (wpos[None, :] > t[:, None] - WINDOW) & (wpos[None, :] >= 0)
        s = jnp.einsum('bqghd,bkgd->bghqk', qb, kwin).astype(f32) * scale
        p = masked_softmax(s, m_win)
        o_win = jnp.einsum('bghqk,bkgd->bqghd', p.astype(dt), vwin)
        o = gb[..., 0:1] * o_cmp + gb[..., 1:2] * o_slc + gb[..., 2:3] * o_win
        return o.astype(dt)

    o = lax.map(block_fn, (jnp.arange(n_qb), q_blocks, g_blocks))
    o = o.transpose(1, 0, 2, 3, 4, 5).reshape(B, T, BRANCH_WIDTH)
    y_seq = o * jax.nn.silu(r)
    y_mem = mem_cross_attention(xq, mem_n, w_kv) * jax.nn.silu(xg)
    return jnp.concatenate([y_seq, y_mem], axis=-1) @ w_out


def setup_inputs(seed: int = 0) -> dict:
    key = jax.random.key(seed)
    ks = jax.random.split(key, 20)
    f32 = jnp.float32

    def nrm(k, shape, scale):
        return jax.random.normal(k, shape, f32) * scale

    nA, nB = N_GLA_LAYERS, N_NSA_LAYERS
    return {
        'x': nrm(ks[0], (BATCH, SEQ, D_MODEL), 1.0),
        'mem': nrm(ks[1], (BATCH, N_MEM, D_MODEL), 1.0),
        'ln_pre': 1.0 + nrm(ks[2], (DEPTH, D_MODEL), 0.02),
        'ln_post': 1.0 + nrm(ks[3], (DEPTH, D_MODEL), 0.02),
        'ln_mem': 1.0 + nrm(ks[4], (DEPTH, D_MODEL), 0.02),
        'w_mem_kv': nrm(ks[5], (DEPTH, D_MODEL, 2 * X_WIDTH), D_MODEL ** -0.5),
        'gla_w_in': nrm(ks[6], (nA, D_MODEL, GLA_IN), D_MODEL ** -0.5),
        'gla_w_gate_lr': nrm(ks[7], (nA, GLA_GATE_RANK, GLA_KEY_WIDTH), GLA_GATE_RANK ** -0.5),
        'gla_b_gate': nrm(ks[8], (nA, GLA_KEY_WIDTH), 0.1),
        'gla_norm': 1.0 + nrm(ks[9], (nA, GLA_DV), 0.02),
        'gla_w_out': nrm(ks[10], (nA, MIX_WIDTH, D_MODEL), MIX_WIDTH ** -0.5),
        'nsa_w_in': nrm(ks[11], (nB, D_MODEL, NSA_IN), D_MODEL ** -0.5),
        'nsa_pe_k': nrm(ks[12], (nB, CMP_LEN, HEAD_DIM), 0.1),
        'nsa_pe_v': nrm(ks[13], (nB, CMP_LEN, HEAD_DIM), 0.1),
        'nsa_wk1': nrm(ks[14], (nB, CMP_LEN * HEAD_DIM, CMP_HIDDEN), (CMP_LEN * HEAD_DIM) ** -0.5),
        'nsa_wk2': nrm(ks[15], (nB, CMP_HIDDEN, HEAD_DIM), CMP_HIDDEN ** -0.5),
        'nsa_wv1': nrm(ks[16], (nB, CMP_LEN * HEAD_DIM, CMP_HIDDEN), (CMP_LEN * HEAD_DIM) ** -0.5),
        'nsa_wv2': nrm(ks[17], (nB, CMP_HIDDEN, HEAD_DIM), CMP_HIDDEN ** -0.5),
        'nsa_w_out': nrm(ks[18], (nB, MIX_WIDTH, D_MODEL), MIX_WIDTH ** -0.5),
    }


def reference(x, mem, ln_pre, ln_post, ln_mem, w_mem_kv, gla_w_in, gla_w_gate_lr, gla_b_gate, gla_norm, gla_w_out,
              nsa_w_in, nsa_pe_k, nsa_pe_v, nsa_wk1, nsa_wk2, nsa_wv1, nsa_wv2, nsa_w_out):
    h = x
    for i in range(DEPTH):
        xn = rmsnorm(h, ln_pre[i])
        mem_n = rmsnorm(mem, ln_mem[i])
        a = i // N_MIXERS
        if i % N_MIXERS == 0:
            y = gla_mixer(xn, mem_n, gla_w_in[a], gla_w_gate_lr[a], gla_b_gate[a], gla_norm[a], w_mem_kv[i], gla_w_out[a])
        else:
            y = nsa_mixer(xn, mem_n, nsa_w_in[a], nsa_pe_k[a], nsa_pe_v[a], nsa_wk1[a], nsa_wk2[a], nsa_wv1[a], nsa_wv2[a],
                          w_mem_kv[i], nsa_w_out[a])
        h = h + rmsnorm(y, ln_post[i])
    return h
```

```python
import functools

import numpy as np
import jax
import jax.numpy as jnp
from jax import lax
from jax.experimental import pallas as pl
from jax.experimental.pallas import tpu as pltpu

F32 = jnp.float32
BF16 = jnp.bfloat16

D_MODEL = 2048
HEAD_DIM = 128
EPS = 1e-6
X_HEADS = 4
X_WIDTH = X_HEADS * HEAD_DIM
BRANCH_WIDTH = D_MODEL - X_WIDTH
GLA_HEADS = 4
GLA_DV = BRANCH_WIDTH // GLA_HEADS
GLA_DK = GLA_DV // 2
GLA_DKP = 256
GLA_RANK = 16
GLA_RANKP = 128
GLA_TEMP = 16.0
GLA_CHUNK = 64
NSA_HEADS = BRANCH_WIDTH // HEAD_DIM
NSA_GROUPS = 2
NSA_HPG = NSA_HEADS // NSA_GROUPS
NSA_GW = NSA_HPG * HEAD_DIM
N_BRANCH = 3
CMP_LEN = 32
CMP_STRIDE = 16
CMP_HIDDEN = 256
SLC_LEN = 64
SLC_TOPN = 16
WINDOW = 512
FORCED = 1e4
NEG = -1e30
ATT_SCALE = HEAD_DIM ** -0.5

VMEM_LIMIT = 56 * 1024 * 1024

GLA_V, GLA_R, GLA_Q, GLA_K, GLA_XQ, GLA_XG, GLA_GLR, GLA_NP = 0, 1536, 3072, 4096, 5120, 5632, 6144, 6400
(NSA_Q, NSA_KC, NSA_VC, NSA_KS, NSA_VS, NSA_KW, NSA_VW, NSA_R, NSA_XQ, NSA_XG, NSA_GL, NSA_NP) = (
    0, 1536, 1792, 2048, 2304, 2560, 2816, 3072, 4608, 5120, 5632, 6144)


def _params(sem):
    return pltpu.CompilerParams(dimension_semantics=sem, vmem_limit_bytes=VMEM_LIMIT)


def _silu(x):
    return x * jax.nn.sigmoid(x)


def _dot_nt(a, b):
    return lax.dot_general(a, b, (((1,), (1,)), ((), ())), preferred_element_type=F32)


def _dot_tn(a, b):
    return lax.dot_general(a, b, (((0,), (0,)), ((), ())), preferred_element_type=F32)


def _split_dot(x, w):
    hi = x.astype(BF16)
    lo = (x - hi.astype(F32)).astype(BF16)
    return (jnp.dot(hi, w, preferred_element_type=F32) + jnp.dot(lo, w, preferred_element_type=F32))


def _norm_proj_body(x_ref, g_ref, w_ref, o_ref, xn_ref):
    @pl.when(pl.program_id(1) == 0)
    def _():
        x = x_ref[...]
        r = lax.rsqrt(jnp.mean(x * x, axis=-1, keepdims=True) + EPS)
        xn_ref[...] = ((x * r) * g_ref[...]).astype(BF16)

    o_ref[...] = jnp.dot(xn_ref[...], w_ref[...], preferred_element_type=F32).astype(o_ref.dtype)


def _norm_proj(x, g, w, tm, tn):
    rows, d = x.shape
    n = w.shape[1]
    return pl.pallas_call(
        _norm_proj_body,
        grid=(rows // tm, n // tn),
        in_specs=[pl.BlockSpec((tm, d), lambda i, j: (i, 0)),
                  pl.BlockSpec((1, d), lambda i, j: (0, 0)),
                  pl.BlockSpec((d, tn), lambda i, j: (0, j))],
        out_specs=pl.BlockSpec((tm, tn), lambda i, j: (i, j)),
        out_shape=jax.ShapeDtypeStruct((rows, n), BF16),
        scratch_shapes=[pltpu.VMEM((tm, d), BF16)],
        compiler_params=_params(("parallel", "arbitrary")),
        name="norm_proj",
    )(x, g.reshape(1, d), w)


def _mem_attn_body(xq_ref, xg_ref, kv_ref, o_ref):
    for h in range(X_HEADS):
        c = slice(h * HEAD_DIM, (h + 1) * HEAD_DIM)
        k = kv_ref[:, c]
        v = kv_ref[:, X_WIDTH + h * HEAD_DIM: X_WIDTH + (h + 1) * HEAD_DIM]
        s = _dot_nt(xq_ref[:, c], k) * ATT_SCALE
        e = jnp.exp(s - jnp.max(s, axis=-1, keepdims=True))
        p = e / jnp.sum(e, axis=-1, keepdims=True)
        o = jnp.dot(p.astype(BF16), v, preferred_element_type=F32)
        o_ref[:, c] = (o * _silu(xg_ref[:, c].astype(F32))).astype(o_ref.dtype)


def _mem_attn(proj, kv, xq_blk, xg_blk, tq):
    t = proj.shape[0]
    m = kv.shape[0]
    return pl.pallas_call(
        _mem_attn_body,
        grid=(t // tq,),
        in_specs=[pl.BlockSpec((tq, X_WIDTH), lambda i: (i, xq_blk)),
                  pl.BlockSpec((tq, X_WIDTH), lambda i: (i, xg_blk)),
                  pl.BlockSpec((m, 2 * X_WIDTH), lambda i: (0, 0))],
        out_specs=pl.BlockSpec((tq, X_WIDTH), lambda i: (i, 0)),
        out_shape=jax.ShapeDtypeStruct((t, X_WIDTH), BF16),
        compiler_params=_params(("parallel",)),
        name="mem_attn",
    )(proj, proj, kv)


def _out_proj_body(ys_ref, ym_ref, ws_ref, wm_ref, g_ref, h_ref, o_ref):
    y = (jnp.dot(ys_ref[...], ws_ref[...], preferred_element_type=F32)
         + jnp.dot(ym_ref[...], wm_ref[...], preferred_element_type=F32))
    r = lax.rsqrt(jnp.mean(y * y, axis=-1, keepdims=True) + EPS)
    o_ref[...] = h_ref[...] + (y * r) * g_ref[...]


def _out_proj(y_seq, y_mem, w_out, g_post, h, tm):
    t, d = h.shape
    ws = w_out[:BRANCH_WIDTH].astype(BF16)
    wm = w_out[BRANCH_WIDTH:].astype(BF16)
    return pl.pallas_call(
        _out_proj_body,
        grid=(t // tm,),
        in_specs=[pl.BlockSpec((tm, BRANCH_WIDTH), lambda i: (i, 0)),
                  pl.BlockSpec((tm, X_WIDTH), lambda i: (i, 0)),
                  pl.BlockSpec((BRANCH_WIDTH, d), lambda i: (0, 0)),
                  pl.BlockSpec((X_WIDTH, d), lambda i: (0, 0)),
                  pl.BlockSpec((1, d), lambda i: (0, 0)),
                  pl.BlockSpec((tm, d), lambda i: (i, 0))],
        out_specs=pl.BlockSpec((tm, d), lambda i: (i, 0)),
        out_shape=jax.ShapeDtypeStruct((t, d), F32),
        compiler_params=_params(("parallel",)),
        name="out_proj",
    )(y_seq, y_mem, ws, wm, g_post.reshape(1, d), h)


def _gla_body(q_ref, k_ref, v_ref, r_ref, glr_ref, wg_ref, bg_ref, gn_ref, tri_ref, y_ref,
              st_ref, b_sc, q_sc, k_sc, v_sc, o_sc, *, chunks):
    C = GLA_CHUNK

    @pl.when(pl.program_id(1) == 0)
    def _():
        st_ref[...] = jnp.zeros_like(st_ref)

    def chunk(c, carry):
        rows = pl.ds(pl.multiple_of(c * C, C), C)
        gp = jnp.dot(glr_ref[rows, :], wg_ref[...], preferred_element_type=F32) + bg_ref[...]
        logg = (jnp.minimum(gp, 0.0) - jnp.log1p(jnp.exp(-jnp.abs(gp)))) * (1.0 / GLA_TEMP)
        b = _split_dot_left(tri_ref[...], logg)
        q = q_ref[rows, :].astype(F32) * (GLA_DK ** -0.5)
        k = k_ref[rows, :].astype(F32)
        v = v_ref[rows, :]
        b_sc[...] = b
        q_sc[...] = q
        k_sc[...] = k
        v_sc[...] = v.astype(F32)
        st = st_ref[...]
        o_sc[...] = _dot_nt((q * jnp.exp(b)).astype(BF16), st.astype(BF16))

        for sb in range(C // 8):
            r0 = sb * 8
            ridx = lax.broadcasted_iota(jnp.int32, (C - r0, 1), 0) + r0

            def step(s8, carry2, r0=r0, ridx=ridx):
                s = r0 + s8
                bs = b_sc[pl.ds(s, 1), :]
                ks = k_sc[pl.ds(s, 1), :]
                vs = v_sc[pl.ds(s, 1), :]
                e = jnp.where(ridx >= s, jnp.exp(b_sc[r0:, :] - bs), 0.0)
                a = jnp.sum(q_sc[r0:, :] * ks * e, axis=-1, keepdims=True)
                o_sc[r0:, :] += a * vs
                return carry2

            lax.fori_loop(0, 8, step, 0)

        bl = b[C - 1:C, :]
        kd = (k * jnp.exp(bl - b)).astype(BF16)
        st_ref[...] = jnp.exp(bl) * st + _dot_tn(v, kd)

        o = o_sc[...]
        on = (o * lax.rsqrt(jnp.mean(o * o, axis=-1, keepdims=True) + EPS)) * gn_ref[...]
        y_ref[rows, :] = (on * _silu(r_ref[rows, :].astype(F32))).astype(y_ref.dtype)
        return carry

    lax.fori_loop(0, chunks, chunk, 0)


def _split_dot_left(w, x):
    hi = x.astype(BF16)
    lo = (x - hi.astype(F32)).astype(BF16)
    return (jnp.dot(w, hi, preferred_element_type=F32) + jnp.dot(w, lo, preferred_element_type=F32))


def _gla_scan(proj, wg, bg, gnorm, tb):
    t = proj.shape[0]
    C = GLA_CHUNK
    tri = jnp.asarray(np.tril(np.ones((C, C), np.float32)), BF16)
    qb, kb, vb, rb, gb = GLA_Q // GLA_DKP, GLA_K // GLA_DKP, GLA_V // GLA_DV, GLA_R // GLA_DV, GLA_GLR // GLA_RANKP
    return pl.pallas_call(
        functools.partial(_gla_body, chunks=tb // C),
        grid=(GLA_HEADS, t // tb),
        in_specs=[pl.BlockSpec((tb, GLA_DKP), lambda h, n: (n, qb + h)),
                  pl.BlockSpec((tb, GLA_DKP), lambda h, n: (n, kb + h)),
                  pl.BlockSpec((tb, GLA_DV), lambda h, n: (n, vb + h)),
                  pl.BlockSpec((tb, GLA_DV), lambda h, n: (n, rb + h)),
                  pl.BlockSpec((tb, GLA_RANKP), lambda h, n: (n, gb)),
                  pl.BlockSpec((None, GLA_RANKP, GLA_DKP), lambda h, n: (h, 0, 0)),
                  pl.BlockSpec((None, 1, GLA_DKP), lambda h, n: (h, 0, 0)),
                  pl.BlockSpec((1, GLA_DV), lambda h, n: (0, 0)),
                  pl.BlockSpec((C, C), lambda h, n: (0, 0))],
        out_specs=pl.BlockSpec((tb, GLA_DV), lambda h, n: (n, h)),
        out_shape=jax.ShapeDtypeStruct((t, BRANCH_WIDTH), BF16),
        scratch_shapes=[pltpu.VMEM((GLA_DV, GLA_DKP), F32),
                        pltpu.VMEM((C, GLA_DKP), F32),
                        pltpu.VMEM((C, GLA_DKP), F32),
                        pltpu.VMEM((C, GLA_DKP), F32),
                        pltpu.VMEM((C, GLA_DV), F32),
                        pltpu.VMEM((C, GLA_DV), F32)],
        compiler_params=_params(("parallel", "arbitrary")),
        name="gla_scan",
    )(proj, proj, proj, proj, proj, wg, bg, gnorm.reshape(1, GLA_DV), tri)


def _pad_heads(w, heads, width, padded):
    d = w.shape[0]
    return jnp.pad(w.reshape(d, heads, width), ((0, 0), (0, 0), (0, padded - width))).reshape(d, heads * padded)


def _gla_layer_seq(xn_proj_fn, w_in, w_gate_lr, b_gate, g_norm):
    sizes = np.cumsum([GLA_HEADS * GLA_DK, GLA_HEADS * GLA_DK, BRANCH_WIDTH, GLA_RANK, BRANCH_WIDTH, X_WIDTH])
    wq, wk, wv, wglr, wr, wxq, wxg = jnp.split(w_in, sizes.tolist(), axis=1)
    w_cat = jnp.concatenate([
        wv, wr, _pad_heads(wq, GLA_HEADS, GLA_DK, GLA_DKP), _pad_heads(wk, GLA_HEADS, GLA_DK, GLA_DKP), wxq, wxg,
        jnp.pad(wglr, ((0, 0), (0, GLA_NP - GLA_GLR - GLA_RANK)))], axis=1).astype(BF16)
    proj = xn_proj_fn(w_cat, 1280)
    wg = jnp.pad(w_gate_lr.reshape(GLA_RANK, GLA_HEADS, GLA_DK).transpose(1, 0, 2),
                 ((0, 0), (0, GLA_RANKP - GLA_RANK), (0, GLA_DKP - GLA_DK))).astype(BF16)
    bg = jnp.pad(b_gate.reshape(GLA_HEADS, 1, GLA_DK), ((0, 0), (0, 0), (0, GLA_DKP - GLA_DK)))
    y_seq = _gla_scan(proj, wg, bg, g_norm, 512)
    return proj, y_seq, GLA_XQ // X_WIDTH, GLA_XG // X_WIDTH


def _cmp_mlp_body(x_ref, pe_ref, w1_ref, w2_ref, o_ref):
    x = (x_ref[...].astype(F32) + pe_ref[...]).astype(BF16)
    hid = _silu(jnp.dot(x, w1_ref[...], preferred_element_type=F32))
    o_ref[...] = jnp.dot(hid.astype(BF16), w2_ref[...], preferred_element_type=F32).astype(o_ref.dtype)


def _cmp_mlp(blocks, pe, w1, w2, tr):
    _, rows, width = blocks.shape
    return pl.pallas_call(
        _cmp_mlp_body,
        grid=(2, rows // tr),
        in_specs=[pl.BlockSpec((None, tr, width), lambda a, i: (a, i, 0)),
                  pl.BlockSpec((None, 1, width), lambda a, i: (a, 0, 0)),
                  pl.BlockSpec((None, width, CMP_HIDDEN), lambda a, i: (a, 0, 0)),
                  pl.BlockSpec((None, CMP_HIDDEN, HEAD_DIM), lambda a, i: (a, 0, 0))],
        out_specs=pl.BlockSpec((None, tr, HEAD_DIM), lambda a, i: (a, i, 0)),
        out_shape=jax.ShapeDtypeStruct((2, rows, HEAD_DIM), BF16),
        compiler_params=_params(("parallel", "parallel")),
        name="cmp_mlp",
    )(blocks, pe, w1, w2)


def _cmp_attn_body(q_ref, kc_ref, vc_ref, ov_ref, o_ref, sel_ref, *, tq, nc, nb):
    i = pl.program_id(0)
    t = i * tq + lax.broadcasted_iota(jnp.int32, (tq, 1), 0)
    n = lax.broadcasted_iota(jnp.int32, (1, nc), 1)
    valid = (n * CMP_STRIDE + (CMP_LEN - 1)) <= t
    jb = lax.broadcasted_iota(jnp.int32, (1, nb), 1)
    cur = jnp.right_shift(t, 6)
    forced = (jb == 0) | (jb == cur) | (jb == cur - 1)
    future = jb > cur
    for g in range(NSA_GROUPS):
        kc = kc_ref[g]
        vc = vc_ref[g]
        imp = jnp.zeros((tq, nc), F32)
        for h in range(NSA_HPG):
            c = slice((g * NSA_HPG + h) * HEAD_DIM, (g * NSA_HPG + h + 1) * HEAD_DIM)
            s = jnp.where(valid, _dot_nt(q_ref[:, c], kc) * ATT_SCALE, NEG)
            e = jnp.exp(s - jnp.max(s, axis=-1, keepdims=True))
            p = jnp.where(valid, e / jnp.sum(e, axis=-1, keepdims=True), 0.0)
            o_ref[:, c] = jnp.dot(p.astype(BF16), vc, preferred_element_type=F32).astype(o_ref.dtype)
            imp = imp + p
        x = _split_dot(imp, ov_ref[...])
        x = jnp.where(forced, FORCED, x)
        x = jnp.where(future, -FORCED, x)

        def pick(_, carry):
            x, sel = carry
            m = jnp.max(x, axis=-1, keepdims=True)
            idx = jnp.min(jnp.where(x == m, jb, nb), axis=-1, keepdims=True)
            hit = jb == idx
            return jnp.where(hit, -3e38, x), jnp.where(hit, 1.0, sel)

        _, sel = lax.fori_loop(0, SLC_TOPN, pick, (x, jnp.zeros((tq, nb), F32)))
        sel_ref[:, g * nb:(g + 1) * nb] = sel.astype(sel_ref.dtype)


def _cmp_attn(proj, kc, vc, overlap, tq):
    t = proj.shape[0]
    nc = kc.shape[1]
    nb = overlap.shape[1]
    return pl.pallas_call(
        functools.partial(_cmp_attn_body, tq=tq, nc=nc, nb=nb),
        grid=(t // tq,),
        in_specs=[pl.BlockSpec((tq, BRANCH_WIDTH), lambda i: (i, NSA_Q // BRANCH_WIDTH)),
                  pl.BlockSpec((NSA_GROUPS, nc, HEAD_DIM), lambda i: (0, 0, 0)),
                  pl.BlockSpec((NSA_GROUPS, nc, HEAD_DIM), lambda i: (0, 0, 0)),
                  pl.BlockSpec((nc, nb), lambda i: (0, 0))],
        out_specs=[pl.BlockSpec((tq, BRANCH_WIDTH), lambda i: (i, 0)),
                   pl.BlockSpec((tq, NSA_GROUPS * nb), lambda i: (i, 0))],
        out_shape=[jax.ShapeDtypeStruct((t, BRANCH_WIDTH), BF16),
                   jax.ShapeDtypeStruct((t, NSA_GROUPS * nb), BF16)],
        compiler_params=_params(("parallel",)),
        name="cmp_attn",
    )(proj, kc, vc, overlap)


def _win_attn_body(q_ref, kp_ref, kcur_ref, vp_ref, vcur_ref, o_ref, *, tq):
    i = pl.program_id(1)
    t = i * tq + lax.broadcasted_iota(jnp.int32, (tq, 1), 0)
    c = lax.broadcasted_iota(jnp.int32, (1, tq), 1)
    pos_p = (i - 1) * tq + c
    pos_c = i * tq + c
    mask_p = (pos_p > t - WINDOW) & (pos_p >= 0)
    mask_c = pos_c <= t
    kp, kcur, vp, vcur = kp_ref[...], kcur_ref[...], vp_ref[...], vcur_ref[...]
    for h in range(NSA_HPG):
        cs = slice(h * HEAD_DIM, (h + 1) * HEAD_DIM)
        q = q_ref[:, cs]
        sp = jnp.where(mask_p, _dot_nt(q, kp) * ATT_SCALE, NEG)
        sc = jnp.where(mask_c, _dot_nt(q, kcur) * ATT_SCALE, NEG)
        m = jnp.maximum(jnp.max(sp, axis=-1, keepdims=True), jnp.max(sc, axis=-1, keepdims=True))
        ep = jnp.where(mask_p, jnp.exp(sp - m), 0.0)
        ec = jnp.where(mask_c, jnp.exp(sc - m), 0.0)
        l = jnp.sum(ep, axis=-1, keepdims=True) + jnp.sum(ec, axis=-1, keepdims=True)
        o = (jnp.dot(ep.astype(BF16), vp, preferred_element_type=F32)
             + jnp.dot(ec.astype(BF16), vcur, preferred_element_type=F32))
        o_ref[:, cs] = (o / l).astype(o_ref.dtype)


def _win_attn(proj, tq):
    t = proj.shape[0]
    kb, vb = NSA_KW // HEAD_DIM, NSA_VW // HEAD_DIM
    prev = lambda g, i: (jnp.maximum(i - 1, 0), kb + g)
    cur = lambda g, i: (i, kb + g)
    prev_v = lambda g, i: (jnp.maximum(i - 1, 0), vb + g)
    cur_v = lambda g, i: (i, vb + g)
    return pl.pallas_call(
        functools.partial(_win_attn_body, tq=tq),
        grid=(NSA_GROUPS, t // tq),
        in_specs=[pl.BlockSpec((tq, NSA_GW), lambda g, i: (i, g)),
                  pl.BlockSpec((tq, HEAD_DIM), prev),
                  pl.BlockSpec((tq, HEAD_DIM), cur),
                  pl.BlockSpec((tq, HEAD_DIM), prev_v),
                  pl.BlockSpec((tq, HEAD_DIM), cur_v)],
        out_specs=pl.BlockSpec((tq, NSA_GW), lambda g, i: (i, g)),
        out_shape=jax.ShapeDtypeStruct((t, BRANCH_WIDTH), BF16),
        compiler_params=_params(("parallel", "parallel")),
        name="win_attn",
    )(proj, proj, proj, proj, proj)


def _slc_attn_body(q_ref, k_ref, v_ref, sel_ref, ocmp_ref, owin_ref, gl_ref, r_ref, y_ref,
                   m_sc, l_sc, acc_sc, *, tq, tk, nb):
    i = pl.program_id(1)
    j = pl.program_id(2)
    last = (i * tq + tq - 1) // tk

    @pl.when(j == 0)
    def _():
        m_sc[...] = jnp.full_like(m_sc, NEG)
        l_sc[...] = jnp.zeros_like(l_sc)
        acc_sc[...] = jnp.zeros_like(acc_sc)

    @pl.when(j <= last)
    def _():
        t = i * tq + lax.broadcasted_iota(jnp.int32, (tq, 1), 0)
        kpos = j * tk + lax.broadcasted_iota(jnp.int32, (1, tk), 1)
        blk = lax.broadcasted_iota(jnp.int32, (nb, 1), 0)
        expand = jnp.where(blk == jnp.right_shift(kpos, 6), 1.0, 0.0).astype(BF16)
        chosen = jnp.dot(sel_ref[...], expand, preferred_element_type=F32) > 0.5
        mask = chosen & (kpos <= t)
        k = k_ref[...]
        v = v_ref[...]
        for h in range(NSA_HPG):
            cs = slice(h * HEAD_DIM, (h + 1) * HEAD_DIM)
            s = jnp.where(mask, _dot_nt(q_ref[:, cs], k) * ATT_SCALE, NEG)
            m_old = m_sc[h]
            m_new = jnp.maximum(m_old, jnp.max(s, axis=-1, keepdims=True))
            alpha = jnp.exp(m_old - m_new)
            p = jnp.where(mask, jnp.exp(s - m_new), 0.0)
            l_sc[h] = alpha * l_sc[h] + jnp.sum(p, axis=-1, keepdims=True)
            acc_sc[h] = alpha * acc_sc[h] + jnp.dot(p.astype(BF16), v, preferred_element_type=F32)
            m_sc[h] = m_new

    @pl.when(j == last)
    def _():
        gates = jax.nn.sigmoid(gl_ref[...].astype(F32))
        for h in range(NSA_HPG):
            cs = slice(h * HEAD_DIM, (h + 1) * HEAD_DIM)
            o_slc = acc_sc[h] / l_sc[h]
            g0 = gates[:, h * N_BRANCH + 0:h * N_BRANCH + 1]
            g1 = gates[:, h * N_BRANCH + 1:h * N_BRANCH + 2]
            g2 = gates[:, h * N_BRANCH + 2:h * N_BRANCH + 3]
            o = g0 * ocmp_ref[:, cs].astype(F32) + g1 * o_slc + g2 * owin_ref[:, cs].astype(F32)
            y_ref[:, cs] = (o * _silu(r_ref[:, cs].astype(F32))).astype(y_ref.dtype)


def _slc_attn(proj, sel, o_cmp, o_win, tq, tk):
    t = proj.shape[0]
    nb = sel.shape[1] // NSA_GROUPS
    kb, vb = NSA_KS // HEAD_DIM, NSA_VS // HEAD_DIM
    last = lambda i: (i * tq + tq - 1) // tk
    return pl.pallas_call(
        functools.partial(_slc_attn_body, tq=tq, tk=tk, nb=nb),
        grid=(NSA_GROUPS, t // tq, t // tk),
        in_specs=[pl.BlockSpec((tq, NSA_GW), lambda g, i, j: (i, g)),
                  pl.BlockSpec((tk, HEAD_DIM), lambda g, i, j: (jnp.minimum(j, last(i)), kb + g)),
                  pl.BlockSpec((tk, HEAD_DIM), lambda g, i, j: (jnp.minimum(j, last(i)), vb + g)),
                  pl.BlockSpec((tq, nb), lambda g, i, j: (i, g)),
                  pl.BlockSpec((tq, NSA_GW), lambda g, i, j: (i, g)),
                  pl.BlockSpec((tq, NSA_GW), lambda g, i, j: (i, g)),
                  pl.BlockSpec((tq, HEAD_DIM), lambda g, i, j: (i, NSA_GL // HEAD_DIM + g)),
                  pl.BlockSpec((tq, NSA_GW), lambda g, i, j: (i, NSA_R // NSA_GW + g))],
        out_specs=pl.BlockSpec((tq, NSA_GW), lambda g, i, j: (i, g)),
        out_shape=jax.ShapeDtypeStruct((t, BRANCH_WIDTH), BF16),
        scratch_shapes=[pltpu.VMEM((NSA_HPG, tq, 1), F32),
                        pltpu.VMEM((NSA_HPG, tq, 1), F32),
                        pltpu.VMEM((NSA_HPG, tq, HEAD_DIM), F32)],
        compiler_params=_params(("parallel", "parallel", "arbitrary")),
        name="slc_attn",
    )(proj, proj, proj, sel, o_cmp, o_win, proj, proj)


def _nsa_layer_seq(xn_proj_fn, w_in, pe_k, pe_v, wk1, wk2, wv1, wv2):
    t_sizes = [BRANCH_WIDTH] + [NSA_GROUPS * HEAD_DIM] * 6 + [NSA_HEADS * N_BRANCH, BRANCH_WIDTH, X_WIDTH]
    parts = jnp.split(w_in, np.cumsum(t_sizes).tolist(), axis=1)
    wq, kv6, wgl, wr, wxq, wxg = parts[0], parts[1:7], parts[7], parts[8], parts[9], parts[10]
    wgl = _pad_heads(wgl, NSA_GROUPS, NSA_HPG * N_BRANCH, HEAD_DIM)
    w_cat = jnp.concatenate([wq] + list(kv6) + [wr, wxq, wxg, wgl], axis=1)
    w_cat = jnp.pad(w_cat, ((0, 0), (0, NSA_NP - w_cat.shape[1]))).astype(BF16)
    proj = xn_proj_fn(w_cat, 1024)
    t = proj.shape[0]

    half = CMP_STRIDE
    nc = t // half
    a = proj[:, NSA_KC:NSA_KS].reshape(nc, half, 2, NSA_GROUPS, HEAD_DIM)
    a = a.transpose(2, 3, 0, 1, 4).reshape(2, NSA_GROUPS, nc, half * HEAD_DIM)
    blocks = jnp.concatenate([a, jnp.roll(a, -1, axis=2)], axis=-1).reshape(2, NSA_GROUPS * nc, CMP_LEN * HEAD_DIM)
    pe = jnp.stack([pe_k, pe_v]).reshape(2, 1, CMP_LEN * HEAD_DIM)
    kv_cmp = _cmp_mlp(blocks, pe, jnp.stack([wk1, wv1]).astype(BF16), jnp.stack([wk2, wv2]).astype(BF16),
                      min(512, NSA_GROUPS * nc))
    kv_cmp = kv_cmp.reshape(2, NSA_GROUPS, nc, HEAD_DIM)

    nb = t // SLC_LEN
    cstart = np.arange(nc) * CMP_STRIDE
    sstart = np.arange(nb) * SLC_LEN
    overlap = ((cstart[:, None] < sstart[None, :] + SLC_LEN) & (cstart[:, None] + CMP_LEN > sstart[None, :]))
    overlap[nc - 1] = False
    o_cmp, sel = _cmp_attn(proj, kv_cmp[0], kv_cmp[1], jnp.asarray(overlap.astype(np.float32), BF16), 128)
    o_win = _win_attn(proj, WINDOW)
    y_seq = _slc_attn(proj, sel, o_cmp, o_win, 512, 512)
    return proj, y_seq, NSA_XQ // X_WIDTH, NSA_XG // X_WIDTH


def kernel(x, mem, ln_pre, ln_post, ln_mem, w_mem_kv, gla_w_in, gla_w_gate_lr, gla_b_gate, gla_norm, gla_w_out,
           nsa_w_in, nsa_pe_k, nsa_pe_v, nsa_wk1, nsa_wk2, nsa_wv1, nsa_wv2, nsa_w_out):
    batch, t, d = x.shape
    assert batch == 1 and d == D_MODEL and t % 1024 == 0
    h = x.reshape(t, d)
    mem2 = mem.reshape(mem.shape[1], d)
    depth = ln_pre.shape[0]
    for i in range(depth):
        a = i // 2
        xn_proj_fn = lambda w, tn, i=i, h=h: _norm_proj(h, ln_pre[i], w, 1024, tn)
        kv = _norm_proj(mem2, ln_mem[i], w_mem_kv[i].astype(BF16), mem2.shape[0], 2 * X_WIDTH)
        if i % 2 == 0:
            proj, y_seq, xq_blk, xg_blk = _gla_layer_seq(xn_proj_fn, gla_w_in[a], gla_w_gate_lr[a], gla_b_gate[a],
                                                         gla_norm[a])
            w_out = gla_w_out[a]
        else:
            proj, y_seq, xq_blk, xg_blk = _nsa_layer_seq(xn_proj_fn, nsa_w_in[a], nsa_pe_k[a], nsa_pe_v[a],
                                                         nsa_wk1[a], nsa_wk2[a], nsa_wv1[a], nsa_wv2[a])
            w_out = nsa_w_out[a]
        y_mem = _mem_attn(proj, kv, xq_blk, xg_blk, 512)
        h = _out_proj(y_seq, y_mem, w_out, ln_post[i], h, 256)
    return h.reshape(batch, t, d)
```

```python
import functools

import numpy as np
import jax
import jax.numpy as jnp
from jax import lax
from jax.experimental import pallas as pl
from jax.experimental.pallas import tpu as pltpu

F32 = jnp.float32
BF16 = jnp.bfloat16

D_MODEL = 2048
HEAD_DIM = 128
EPS = 1e-6
X_HEADS = 4
X_WIDTH = X_HEADS * HEAD_DIM
BRANCH_WIDTH = D_MODEL - X_WIDTH
GLA_HEADS = 4
GLA_DV = BRANCH_WIDTH // GLA_HEADS
GLA_DK = GLA_DV // 2
GLA_DKP = 256
GLA_RANK = 16
GLA_RANKP = 128
GLA_TEMP = 16.0
GLA_CHUNK = 64
GLA_SUB = 16
NSA_HEADS = BRANCH_WIDTH // HEAD_DIM
NSA_GROUPS = 2
NSA_HPG = NSA_HEADS // NSA_GROUPS
NSA_GW = NSA_HPG * HEAD_DIM
N_BRANCH = 3
CMP_LEN = 32
CMP_STRIDE = 16
CMP_HIDDEN = 256
SLC_LEN = 64
SLC_TOPN = 16
WINDOW = 512
FORCED = 1e4
NEG = -1e30
ATT_SCALE = HEAD_DIM ** -0.5
LOG2E = 1.4426950408889634

VMEM_LIMIT = 56 * 1024 * 1024

GLA_V, GLA_R, GLA_Q, GLA_K, GLA_XQ, GLA_XG, GLA_GLR, GLA_NP = 0, 1536, 3072, 4096, 5120, 5632, 6144, 6400
(NSA_Q, NSA_KC, NSA_VC, NSA_KS, NSA_VS, NSA_KW, NSA_VW, NSA_R, NSA_XQ, NSA_XG, NSA_GL, NSA_NP) = (
    0, 1536, 1792, 2048, 2304, 2560, 2816, 3072, 4608, 5120, 5632, 6144)


def _params(sem):
    return pltpu.CompilerParams(dimension_semantics=sem, vmem_limit_bytes=VMEM_LIMIT)


def _silu(x):
    return x * jax.nn.sigmoid(x)


def _dot_nt(a, b):
    return lax.dot_general(a, b, (((1,), (1,)), ((), ())), preferred_element_type=F32)


def _dot_tn(a, b):
    return lax.dot_general(a, b, (((0,), (0,)), ((), ())), preferred_element_type=F32)


def _split_dot(x, w):
    hi = x.astype(BF16)
    lo = (x - hi.astype(F32)).astype(BF16)
    return (jnp.dot(hi, w, preferred_element_type=F32) + jnp.dot(lo, w, preferred_element_type=F32))


def _norm_proj_body(x_ref, g_ref, w_ref, o_ref, xn_ref):
    @pl.when(pl.program_id(1) == 0)
    def _():
        x = x_ref[...]
        r = lax.rsqrt(jnp.mean(x * x, axis=-1, keepdims=True) + EPS)
        xn_ref[...] = ((x * r) * g_ref[...]).astype(BF16)

    o_ref[...] = jnp.dot(xn_ref[...], w_ref[...], preferred_element_type=F32).astype(o_ref.dtype)


def _norm_proj(x, g, w, tm, tn):
    rows, d = x.shape
    n = w.shape[1]
    return pl.pallas_call(
        _norm_proj_body,
        grid=(rows // tm, n // tn),
        in_specs=[pl.BlockSpec((tm, d), lambda i, j: (i, 0)),
                  pl.BlockSpec((1, d), lambda i, j: (0, 0)),
                  pl.BlockSpec((d, tn), lambda i, j: (0, j))],
        out_specs=pl.BlockSpec((tm, tn), lambda i, j: (i, j)),
        out_shape=jax.ShapeDtypeStruct((rows, n), BF16),
        scratch_shapes=[pltpu.VMEM((tm, d), BF16)],
        compiler_params=_params(("parallel", "arbitrary")),
        name="norm_proj",
    )(x, g.reshape(1, d), w)


def _mem_attn_body(xq_ref, xg_ref, kv_ref, o_ref):
    for h in range(X_HEADS):
        c = slice(h * HEAD_DIM, (h + 1) * HEAD_DIM)
        k = kv_ref[:, c]
        v = kv_ref[:, X_WIDTH + h * HEAD_DIM: X_WIDTH + (h + 1) * HEAD_DIM]
        s = _dot_nt(xq_ref[:, c], k) * ATT_SCALE
        e = jnp.exp(s - jnp.max(s, axis=-1, keepdims=True))
        p = e / jnp.sum(e, axis=-1, keepdims=True)
        o = jnp.dot(p.astype(BF16), v, preferred_element_type=F32)
        o_ref[:, c] = (o * _silu(xg_ref[:, c].astype(F32))).astype(o_ref.dtype)


def _mem_attn(proj, kv, xq_blk, xg_blk, tq):
    t = proj.shape[0]
    m = kv.shape[0]
    return pl.pallas_call(
        _mem_attn_body,
        grid=(t // tq,),
        in_specs=[pl.BlockSpec((tq, X_WIDTH), lambda i: (i, xq_blk)),
                  pl.BlockSpec((tq, X_WIDTH), lambda i: (i, xg_blk)),
                  pl.BlockSpec((m, 2 * X_WIDTH), lambda i: (0, 0))],
        out_specs=pl.BlockSpec((tq, X_WIDTH), lambda i: (i, 0)),
        out_shape=jax.ShapeDtypeStruct((t, X_WIDTH), BF16),
        compiler_params=_params(("parallel",)),
        name="mem_attn",
    )(proj, proj, kv)


def _out_proj_body(ys_ref, ym_ref, ws_ref, wm_ref, g_ref, h_ref, o_ref):
    y = (jnp.dot(ys_ref[...], ws_ref[...], preferred_element_type=F32)
         + jnp.dot(ym_ref[...], wm_ref[...], preferred_element_type=F32))
    r = lax.rsqrt(jnp.mean(y * y, axis=-1, keepdims=True) + EPS)
    o_ref[...] = h_ref[...] + (y * r) * g_ref[...]


def _out_proj(y_seq, y_mem, w_out, g_post, h, tm):
    t, d = h.shape
    ws = w_out[:BRANCH_WIDTH].astype(BF16)
    wm = w_out[BRANCH_WIDTH:].astype(BF16)
    return pl.pallas_call(
        _out_proj_body,
        grid=(t // tm,),
        in_specs=[pl.BlockSpec((tm, BRANCH_WIDTH), lambda i: (i, 0)),
                  pl.BlockSpec((tm, X_WIDTH), lambda i: (i, 0)),
                  pl.BlockSpec((BRANCH_WIDTH, d), lambda i: (0, 0)),
                  pl.BlockSpec((X_WIDTH, d), lambda i: (0, 0)),
                  pl.BlockSpec((1, d), lambda i: (0, 0)),
                  pl.BlockSpec((tm, d), lambda i: (i, 0))],
        out_specs=pl.BlockSpec((tm, d), lambda i: (i, 0)),
        out_shape=jax.ShapeDtypeStruct((t, d), F32),
        compiler_params=_params(("parallel",)),
        name="out_proj",
    )(y_seq, y_mem, ws, wm, g_post.reshape(1, d), h)


def _gla_body(q_ref, k_ref, v_ref, r_ref, glr_ref, wg_ref, bg_ref, gn_ref, tri_ref, y_ref,
              st_ref, b_sc, q_sc, k_sc, v_sc, o_sc, *, chunks):
    C = GLA_CHUNK

    @pl.when(pl.program_id(1) == 0)
    def _():
        st_ref[...] = jnp.zeros_like(st_ref)

    def chunk(c, carry):
        rows = pl.ds(pl.multiple_of(c * C, C), C)
        gp = jnp.dot(glr_ref[rows, :], wg_ref[...], preferred_element_type=F32) + bg_ref[...]
        logg = (jnp.minimum(gp, 0.0) - jnp.log1p(jnp.exp(-jnp.abs(gp)))) * (1.0 / GLA_TEMP)
        b = _split_dot_left(tri_ref[...], logg)
        q = q_ref[rows, :].astype(F32) * (GLA_DK ** -0.5)
        k = k_ref[rows, :].astype(F32)
        v = v_ref[rows, :]
        b_sc[...] = b
        q_sc[...] = q
        k_sc[...] = k
        v_sc[...] = v.astype(F32)
        st = st_ref[...]
        o_sc[...] = _dot_nt((q * jnp.exp(b)).astype(BF16), st.astype(BF16))

        SB = GLA_SUB
        for j in range(C // SB - 1):
            lo, hi = j * SB, (j + 1) * SB
            beta = b_sc[hi:hi + 1, :]
            qt = (q_sc[hi:, :] * jnp.exp(b_sc[hi:, :] - beta)).astype(BF16)
            kt = (k_sc[lo:hi, :] * jnp.exp(beta - b_sc[lo:hi, :])).astype(BF16)
            a = _dot_nt(qt, kt)
            vj = v_ref[pl.ds(pl.multiple_of(c * C + lo, SB), SB), :]
            o_sc[hi:, :] += jnp.dot(a.astype(BF16), vj, preferred_element_type=F32)

        sub8 = lax.broadcasted_iota(jnp.int32, (8, 1), 0)
        for blk in range(C // SB):
            base = blk * SB
            acc = [jnp.zeros((8, GLA_DV), F32) for _ in range(SB // 8)]
            for s in range(SB):
                row = base + s
                bs = b_sc[row:row + 1, :]
                ks = k_sc[row:row + 1, :]
                vs = v_sc[row:row + 1, :]
                for tix in range(s // 8, SB // 8):
                    r0 = base + tix * 8
                    e = jnp.exp(b_sc[r0:r0 + 8, :] - bs)
                    if tix == s // 8:
                        e = jnp.where(sub8 >= s % 8, e, 0.0)
                    a = jnp.sum(q_sc[r0:r0 + 8, :] * ks * e, axis=-1, keepdims=True)
                    acc[tix] = acc[tix] + a * vs
            for tix in range(SB // 8):
                r0 = base + tix * 8
                o_sc[r0:r0 + 8, :] += acc[tix]

        bl = b[C - 1:C, :]
        kd = (k * jnp.exp(bl - b)).astype(BF16)
        st_ref[...] = jnp.exp(bl) * st + _dot_tn(v, kd)

        o = o_sc[...]
        on = (o * lax.rsqrt(jnp.mean(o * o, axis=-1, keepdims=True) + EPS)) * gn_ref[...]
        y_ref[rows, :] = (on * _silu(r_ref[rows, :].astype(F32))).astype(y_ref.dtype)
        return carry

    lax.fori_loop(0, chunks, chunk, 0)


def _split_dot_left(w, x):
    hi = x.astype(BF16)
    lo = (x - hi.astype(F32)).astype(BF16)
    return (jnp.dot(w, hi, preferred_element_type=F32) + jnp.dot(w, lo, preferred_element_type=F32))


def _gla_scan(proj, wg, bg, gnorm, tb):
    t = proj.shape[0]
    C = GLA_CHUNK
    tri = jnp.asarray(np.tril(np.ones((C, C), np.float32)), BF16)
    qb, kb, vb, rb, gb = GLA_Q // GLA_DKP, GLA_K // GLA_DKP, GLA_V // GLA_DV, GLA_R // GLA_DV, GLA_GLR // GLA_RANKP
    return pl.pallas_call(
        functools.partial(_gla_body, chunks=tb // C),
        grid=(GLA_HEADS, t // tb),
        in_specs=[pl.BlockSpec((tb, GLA_DKP), lambda h, n: (n, qb + h)),
                  pl.BlockSpec((tb, GLA_DKP), lambda h, n: (n, kb + h)),
                  pl.BlockSpec((tb, GLA_DV), lambda h, n: (n, vb + h)),
                  pl.BlockSpec((tb, GLA_DV), lambda h, n: (n, rb + h)),
                  pl.BlockSpec((tb, GLA_RANKP), lambda h, n: (n, gb)),
                  pl.BlockSpec((None, GLA_RANKP, GLA_DKP), lambda h, n: (h, 0, 0)),
                  pl.BlockSpec((None, 1, GLA_DKP), lambda h, n: (h, 0, 0)),
                  pl.BlockSpec((1, GLA_DV), lambda h, n: (0, 0)),
                  pl.BlockSpec((C, C), lambda h, n: (0, 0))],
        out_specs=pl.BlockSpec((tb, GLA_DV), lambda h, n: (n, h)),
        out_shape=jax.ShapeDtypeStruct((t, BRANCH_WIDTH), BF16),
        scratch_shapes=[pltpu.VMEM((GLA_DV, GLA_DKP), F32),
                        pltpu.VMEM((C, GLA_DKP), F32),
                        pltpu.VMEM((C, GLA_DKP), F32),
                        pltpu.VMEM((C, GLA_DKP), F32),
                        pltpu.VMEM((C, GLA_DV), F32),
                        pltpu.VMEM((C, GLA_DV), F32)],
        compiler_params=_params(("parallel", "arbitrary")),
        name="gla_scan",
    )(proj, proj, proj, proj, proj, wg, bg, gnorm.reshape(1, GLA_DV), tri)


def _pad_heads(w, heads, width, padded):
    d = w.shape[0]
    return jnp.pad(w.reshape(d, heads, width), ((0, 0), (0, 0), (0, padded - width))).reshape(d, heads * padded)


def _gla_layer_seq(xn_proj_fn, w_in, w_gate_lr, b_gate, g_norm):
    sizes = np.cumsum([GLA_HEADS * GLA_DK, GLA_HEADS * GLA_DK, BRANCH_WIDTH, GLA_RANK, BRANCH_WIDTH, X_WIDTH])
    wq, wk, wv, wglr, wr, wxq, wxg = jnp.split(w_in, sizes.tolist(), axis=1)
    w_cat = jnp.concatenate([
        wv, wr, _pad_heads(wq, GLA_HEADS, GLA_DK, GLA_DKP), _pad_heads(wk, GLA_HEADS, GLA_DK, GLA_DKP), wxq, wxg,
        jnp.pad(wglr, ((0, 0), (0, GLA_NP - GLA_GLR - GLA_RANK)))], axis=1).astype(BF16)
    proj = xn_proj_fn(w_cat, 1280)
    wg = jnp.pad(w_gate_lr.reshape(GLA_RANK, GLA_HEADS, GLA_DK).transpose(1, 0, 2),
                 ((0, 0), (0, GLA_RANKP - GLA_RANK), (0, GLA_DKP - GLA_DK))).astype(BF16)
    bg = jnp.pad(b_gate.reshape(GLA_HEADS, 1, GLA_DK), ((0, 0), (0, 0), (0, GLA_DKP - GLA_DK)))
    y_seq = _gla_scan(proj, wg, bg, g_norm, 512)
    return proj, y_seq, GLA_XQ // X_WIDTH, GLA_XG // X_WIDTH


def _cmp_mlp_body(x_ref, pe_ref, w1_ref, w2_ref, o_ref):
    x = (x_ref[...].astype(F32) + pe_ref[...]).astype(BF16)
    hid = _silu(jnp.dot(x, w1_ref[...], preferred_element_type=F32))
    o_ref[...] = jnp.dot(hid.astype(BF16), w2_ref[...], preferred_element_type=F32).astype(o_ref.dtype)


def _cmp_mlp(blocks, pe, w1, w2, tr):
    _, rows, width = blocks.shape
    return pl.pallas_call(
        _cmp_mlp_body,
        grid=(2, rows // tr),
        in_specs=[pl.BlockSpec((None, tr, width), lambda a, i: (a, i, 0)),
                  pl.BlockSpec((None, 1, width), lambda a, i: (a, 0, 0)),
                  pl.BlockSpec((None, width, CMP_HIDDEN), lambda a, i: (a, 0, 0)),
                  pl.BlockSpec((None, CMP_HIDDEN, HEAD_DIM), lambda a, i: (a, 0, 0))],
        out_specs=pl.BlockSpec((None, tr, HEAD_DIM), lambda a, i: (a, i, 0)),
        out_shape=jax.ShapeDtypeStruct((2, rows, HEAD_DIM), BF16),
        compiler_params=_params(("parallel", "parallel")),
        name="cmp_mlp",
    )(blocks, pe, w1, w2)


def _cmp_attn_body(q_ref, kc_ref, vc_ref, ov_ref, o_ref, sel_ref, *, tq, nc, nb):
    i = pl.program_id(0)
    t = i * tq + lax.broadcasted_iota(jnp.int32, (tq, 1), 0)
    n = lax.broadcasted_iota(jnp.int32, (1, nc), 1)
    valid = (n * CMP_STRIDE + (CMP_LEN - 1)) <= t
    jb = lax.broadcasted_iota(jnp.int32, (1, nb), 1)
    cur = jnp.right_shift(t, 6)
    forced = (jb == 0) | (jb == cur) | (jb == cur - 1)
    future = jb > cur
    for g in range(NSA_GROUPS):
        kc = kc_ref[g]
        vc = vc_ref[g]
        imp = jnp.zeros((tq, nc), F32)
        for h in range(NSA_HPG):
            c = slice((g * NSA_HPG + h) * HEAD_DIM, (g * NSA_HPG + h + 1) * HEAD_DIM)
            s = jnp.where(valid, _dot_nt(q_ref[:, c], kc) * ATT_SCALE, NEG)
            e = jnp.exp(s - jnp.max(s, axis=-1, keepdims=True))
            p = jnp.where(valid, e / jnp.sum(e, axis=-1, keepdims=True), 0.0)
            o_ref[:, c] = jnp.dot(p.astype(BF16), vc, preferred_element_type=F32).astype(o_ref.dtype)
            imp = imp + p
        x = _split_dot(imp, ov_ref[...])
        x = jnp.where(forced, FORCED, x)
        x = jnp.where(future, -FORCED, x)

        def pick(_, carry):
            x, sel = carry
            m = jnp.max(x, axis=-1, keepdims=True)
            idx = jnp.min(jnp.where(x == m, jb, nb), axis=-1, keepdims=True)
            hit = jb == idx
            return jnp.where(hit, -3e38, x), jnp.where(hit, 0.0, sel)

        _, sel = lax.fori_loop(0, SLC_TOPN, pick, (x, jnp.full((tq, nb), NEG, F32)))
        sel_ref[:, g * nb:(g + 1) * nb] = sel.astype(sel_ref.dtype)


def _cmp_attn(proj, kc, vc, overlap, tq):
    t = proj.shape[0]
    nc = kc.shape[1]
    nb = overlap.shape[1]
    return pl.pallas_call(
        functools.partial(_cmp_attn_body, tq=tq, nc=nc, nb=nb),
        grid=(t // tq,),
        in_specs=[pl.BlockSpec((tq, BRANCH_WIDTH), lambda i: (i, NSA_Q // BRANCH_WIDTH)),
                  pl.BlockSpec((NSA_GROUPS, nc, HEAD_DIM), lambda i: (0, 0, 0)),
                  pl.BlockSpec((NSA_GROUPS, nc, HEAD_DIM), lambda i: (0, 0, 0)),
                  pl.BlockSpec((nc, nb), lambda i: (0, 0))],
        out_specs=[pl.BlockSpec((tq, BRANCH_WIDTH), lambda i: (i, 0)),
                   pl.BlockSpec((tq, NSA_GROUPS * nb), lambda i: (i, 0))],
        out_shape=[jax.ShapeDtypeStruct((t, BRANCH_WIDTH), BF16),
                   jax.ShapeDtypeStruct((t, NSA_GROUPS * nb), BF16)],
        compiler_params=_params(("parallel",)),
        name="cmp_attn",
    )(proj, kc, vc, overlap)


def _win_attn_body(q_ref, kp_ref, kcur_ref, vp_ref, vcur_ref, o_ref, *, tq):
    i = pl.program_id(1)
    t = i * tq + lax.broadcasted_iota(jnp.int32, (tq, 1), 0)
    c = lax.broadcasted_iota(jnp.int32, (1, tq), 1)
    pos_p = (i - 1) * tq + c
    pos_c = i * tq + c
    mask_p = (pos_p > t - WINDOW) & (pos_p >= 0)
    mask_c = pos_c <= t
    kp, kcur, vp, vcur = kp_ref[...], kcur_ref[...], vp_ref[...], vcur_ref[...]
    for h in range(NSA_HPG):
        cs = slice(h * HEAD_DIM, (h + 1) * HEAD_DIM)
        q = q_ref[:, cs]
        sp = jnp.where(mask_p, _dot_nt(q, kp) * ATT_SCALE, NEG)
        sc = jnp.where(mask_c, _dot_nt(q, kcur) * ATT_SCALE, NEG)
        m = jnp.maximum(jnp.max(sp, axis=-1, keepdims=True), jnp.max(sc, axis=-1, keepdims=True))
        ep = jnp.where(mask_p, jnp.exp(sp - m), 0.0)
        ec = jnp.where(mask_c, jnp.exp(sc - m), 0.0)
        l = jnp.sum(ep, axis=-1, keepdims=True) + jnp.sum(ec, axis=-1, keepdims=True)
        o = (jnp.dot(ep.astype(BF16), vp, preferred_element_type=F32)
             + jnp.dot(ec.astype(BF16), vcur, preferred_element_type=F32))
        o_ref[:, cs] = (o / l).astype(o_ref.dtype)


def _win_attn(proj, tq):
    t = proj.shape[0]
    kb, vb = NSA_KW // HEAD_DIM, NSA_VW // HEAD_DIM
    prev = lambda g, i: (jnp.maximum(i - 1, 0), kb + g)
    cur = lambda g, i: (i, kb + g)
    prev_v = lambda g, i: (jnp.maximum(i - 1, 0), vb + g)
    cur_v = lambda g, i: (i, vb + g)
    return pl.pallas_call(
        functools.partial(_win_attn_body, tq=tq),
        grid=(NSA_GROUPS, t // tq),
        in_specs=[pl.BlockSpec((tq, NSA_GW), lambda g, i: (i, g)),
                  pl.BlockSpec((tq, HEAD_DIM), prev),
                  pl.BlockSpec((tq, HEAD_DIM), cur),
                  pl.BlockSpec((tq, HEAD_DIM), prev_v),
                  pl.BlockSpec((tq, HEAD_DIM), cur_v)],
        out_specs=pl.BlockSpec((tq, NSA_GW), lambda g, i: (i, g)),
        out_shape=jax.ShapeDtypeStruct((t, BRANCH_WIDTH), BF16),
        compiler_params=_params(("parallel", "parallel")),
        name="win_attn",
    )(proj, proj, proj, proj, proj)


def _slc_attn_body(it_ref, jt_ref, q_ref, k_ref, v_ref, sel_ref, ocmp_ref, owin_ref, gl_ref, r_ref, y_ref,
                   qa_sc, m_sc, acc_sc, *, tq, tk):
    step = pl.program_id(1)
    i = it_ref[step]
    j = jt_ref[step]

    @pl.when(j == 0)
    def _():
        for h in range(NSA_HPG):
            cs = slice(h * HEAD_DIM, (h + 1) * HEAD_DIM)
            qa_sc[h, :, :HEAD_DIM] = (q_ref[:, cs].astype(F32) * (ATT_SCALE * LOG2E)).astype(BF16)
        m_sc[...] = jnp.full_like(m_sc, NEG)
        acc_sc[...] = jnp.zeros_like(acc_sc)

    sel = sel_ref[...]
    for h in range(NSA_HPG):
        qa_sc[h, :, HEAD_DIM:] = sel

    def accumulate(diag):
        k = k_ref[...]
        v = v_ref[...]
        if diag:
            t = lax.broadcasted_iota(jnp.int32, (tq, 1), 0)
            kpos = lax.broadcasted_iota(jnp.int32, (1, tk), 1)
            causal = jnp.where(kpos <= t, 0.0, NEG)
        for h in range(NSA_HPG):
            s = _dot_nt(qa_sc[h], k)
            if diag:
                s = s + causal
            m_old = m_sc[h]
            m_new = jnp.maximum(m_old, jnp.max(s, axis=-1, keepdims=True))
            p = jnp.exp2(s - m_new)
            acc_sc[h] = jnp.exp2(m_old - m_new) * acc_sc[h] + jnp.dot(p.astype(BF16), v, preferred_element_type=F32)
            m_sc[h] = m_new

    @pl.when(j < i)
    def _():
        accumulate(False)

    @pl.when(j == i)
    def _():
        accumulate(True)
        gates = jax.nn.sigmoid(gl_ref[...].astype(F32))
        for h in range(NSA_HPG):
            cs = slice(h * HEAD_DIM, (h + 1) * HEAD_DIM)
            o_slc = acc_sc[h, :, :HEAD_DIM] / acc_sc[h, :, HEAD_DIM:]
            g0 = gates[:, h * N_BRANCH + 0:h * N_BRANCH + 1]
            g1 = gates[:, h * N_BRANCH + 1:h * N_BRANCH + 2]
            g2 = gates[:, h * N_BRANCH + 2:h * N_BRANCH + 3]
            o = g0 * ocmp_ref[:, cs].astype(F32) + g1 * o_slc + g2 * owin_ref[:, cs].astype(F32)
            y_ref[:, cs] = (o * _silu(r_ref[:, cs].astype(F32))).astype(y_ref.dtype)


def _slc_attn(proj, ka, va, sel, o_cmp, o_win, tile):
    t = proj.shape[0]
    nb = sel.shape[1] // NSA_GROUPS
    lw = ka.shape[2] - HEAD_DIM
    ni = t // tile
    it = np.concatenate([np.full(i + 1, i) for i in range(ni)]).astype(np.int32)
    jt = np.concatenate([np.arange(i + 1) for i in range(ni)]).astype(np.int32)
    blocks_per_tile = tile // SLC_LEN
    sel_idx = lambda g, s, it, jt: (it[s], g * (nb // lw) + (jt[s] * blocks_per_tile) // lw)
    row = lambda g, s, it, jt: (it[s], g)
    grid_spec = pltpu.PrefetchScalarGridSpec(
        num_scalar_prefetch=2,
        grid=(NSA_GROUPS, len(it)),
        in_specs=[pl.BlockSpec((tile, NSA_GW), row),
                  pl.BlockSpec((None, tile, HEAD_DIM + lw), lambda g, s, it, jt: (g, jt[s], 0)),
                  pl.BlockSpec((None, tile, 2 * HEAD_DIM), lambda g, s, it, jt: (g, jt[s], 0)),
                  pl.BlockSpec((tile, lw), sel_idx),
                  pl.BlockSpec((tile, NSA_GW), row),
                  pl.BlockSpec((tile, NSA_GW), row),
                  pl.BlockSpec((tile, HEAD_DIM), lambda g, s, it, jt: (it[s], NSA_GL // HEAD_DIM + g)),
                  pl.BlockSpec((tile, NSA_GW), lambda g, s, it, jt: (it[s], NSA_R // NSA_GW + g))],
        out_specs=pl.BlockSpec((tile, NSA_GW), row),
        scratch_shapes=[pltpu.VMEM((NSA_HPG, tile, HEAD_DIM + lw), BF16),
                        pltpu.VMEM((NSA_HPG, tile, 1), F32),
                        pltpu.VMEM((NSA_HPG, tile, 2 * HEAD_DIM), F32)])
    return pl.pallas_call(
        functools.partial(_slc_attn_body, tq=tile, tk=tile),
        grid_spec=grid_spec,
        out_shape=jax.ShapeDtypeStruct((t, BRANCH_WIDTH), BF16),
        compiler_params=_params(("parallel", "arbitrary")),
        name="slc_attn",
    )(jnp.asarray(it), jnp.asarray(jt), proj, ka, va, sel, o_cmp, o_win, proj, proj)


def _nsa_layer_seq(xn_proj_fn, w_in, pe_k, pe_v, wk1, wk2, wv1, wv2):
    t_sizes = [BRANCH_WIDTH] + [NSA_GROUPS * HEAD_DIM] * 6 + [NSA_HEADS * N_BRANCH, BRANCH_WIDTH, X_WIDTH]
    parts = jnp.split(w_in, np.cumsum(t_sizes).tolist(), axis=1)
    wq, kv6, wgl, wr, wxq, wxg = parts[0], parts[1:7], parts[7], parts[8], parts[9], parts[10]
    wgl = _pad_heads(wgl, NSA_GROUPS, NSA_HPG * N_BRANCH, HEAD_DIM)
    w_cat = jnp.concatenate([wq] + list(kv6) + [wr, wxq, wxg, wgl], axis=1)
    w_cat = jnp.pad(w_cat, ((0, 0), (0, NSA_NP - w_cat.shape[1]))).astype(BF16)
    proj = xn_proj_fn(w_cat, 1024)
    t = proj.shape[0]

    half = CMP_STRIDE
    nc = t // half
    a = proj[:, NSA_KC:NSA_KS].reshape(nc, half, 2, NSA_GROUPS, HEAD_DIM)
    a = a.transpose(2, 3, 0, 1, 4).reshape(2, NSA_GROUPS, nc, half * HEAD_DIM)
    blocks = jnp.concatenate([a, jnp.roll(a, -1, axis=2)], axis=-1).reshape(2, NSA_GROUPS * nc, CMP_LEN * HEAD_DIM)
    pe = jnp.stack([pe_k, pe_v]).reshape(2, 1, CMP_LEN * HEAD_DIM)
    kv_cmp = _cmp_mlp(blocks, pe, jnp.stack([wk1, wv1]).astype(BF16), jnp.stack([wk2, wv2]).astype(BF16),
                      min(512, NSA_GROUPS * nc))
    kv_cmp = kv_cmp.reshape(2, NSA_GROUPS, nc, HEAD_DIM)

    nb = t // SLC_LEN
    cstart = np.arange(nc) * CMP_STRIDE
    sstart = np.arange(nb) * SLC_LEN
    overlap = ((cstart[:, None] < sstart[None, :] + SLC_LEN) & (cstart[:, None] + CMP_LEN > sstart[None, :]))
    overlap[nc - 1] = False
    o_cmp, sel = _cmp_attn(proj, kv_cmp[0], kv_cmp[1], jnp.asarray(overlap.astype(np.float32), BF16), 128)
    o_win = _win_attn(proj, WINDOW)
    lw = min(HEAD_DIM, nb)
    onehot = (np.arange(t)[:, None] // SLC_LEN % lw == np.arange(lw)[None, :]).astype(np.float32)
    ks = proj[:, NSA_KS:NSA_VS].reshape(t, NSA_GROUPS, HEAD_DIM).transpose(1, 0, 2)
    vs = proj[:, NSA_VS:NSA_KW].reshape(t, NSA_GROUPS, HEAD_DIM).transpose(1, 0, 2)
    ka = jnp.concatenate([ks, jnp.broadcast_to(jnp.asarray(onehot, BF16), (NSA_GROUPS, t, lw))], axis=-1)
    va = jnp.concatenate([vs, jnp.ones((NSA_GROUPS, t, HEAD_DIM), BF16)], axis=-1)
    y_seq = _slc_attn(proj, ka, va, sel, o_cmp, o_win, 512)
    return proj, y_seq, NSA_XQ // X_WIDTH, NSA_XG // X_WIDTH


def kernel(x, mem, ln_pre, ln_post, ln_mem, w_mem_kv, gla_w_in, gla_w_gate_lr, gla_b_gate, gla_norm, gla_w_out,
           nsa_w_in, nsa_pe_k, nsa_pe_v, nsa_wk1, nsa_wk2, nsa_wv1, nsa_wv2, nsa_w_out):
    batch, t, d = x.shape
    assert batch == 1 and d == D_MODEL and t % 1024 == 0
    h = x.reshape(t, d)
    mem2 = mem.reshape(mem.shape[1], d)
    depth = ln_pre.shape[0]
    for i in range(depth):
        a = i // 2
        xn_proj_fn = lambda w, tn, i=i, h=h: _norm_proj(h, ln_pre[i], w, 1024, tn)
        kv = _norm_proj(mem2, ln_mem[i], w_mem_kv[i].astype(BF16), mem2.shape[0], 2 * X_WIDTH)
        if i % 2 == 0:
            proj, y_seq, xq_blk, xg_blk = _gla_layer_seq(xn_proj_fn, gla_w_in[a], gla_w_gate_lr[a], gla_b_gate[a],
                                                         gla_norm[a])
            w_out = gla_w_out[a]
        else:
            proj, y_seq, xq_blk, xg_blk = _nsa_layer_seq(xn_proj_fn, nsa_w_in[a], nsa_pe_k[a], nsa_pe_v[a],
                                                         nsa_wk1[a], nsa_wk2[a], nsa_wv1[a], nsa_wv2[a])
            w_out = nsa_w_out[a]
        y_mem = _mem_attn(proj, kv, xq_blk, xg_blk, 512)
        h = _out_proj(y_seq, y_mem, w_out, ln_post[i], h, 256)
    return h.reshape(batch, t, d)
```

```python
import functools

import numpy as np
import jax
import jax.numpy as jnp
from jax import lax
from jax.experimental import pallas as pl
from jax.experimental.pallas import tpu as pltpu

F32 = jnp.float32
BF16 = jnp.bfloat16

D_MODEL = 2048
HEAD_DIM = 128
EPS = 1e-6
X_HEADS = 4
X_WIDTH = X_HEADS * HEAD_DIM
BRANCH_WIDTH = D_MODEL - X_WIDTH
GLA_HEADS = 4
GLA_DV = BRANCH_WIDTH // GLA_HEADS
GLA_DK = GLA_DV // 2
GLA_DKP = 256
GLA_RANK = 16
GLA_RANKP = 128
GLA_TEMP = 16.0
GLA_CHUNK = 64
GLA_SUB = 16
SLC_ROWS = 128
CMP_ROWS = 128
WIN_ROWS = 128
TOPK_ROWS = 128
PICKED = -3e38
LANES = 128
NSA_HEADS = BRANCH_WIDTH // HEAD_DIM
NSA_GROUPS = 2
NSA_HPG = NSA_HEADS // NSA_GROUPS
NSA_GW = NSA_HPG * HEAD_DIM
N_BRANCH = 3
CMP_LEN = 32
CMP_STRIDE = 16
CMP_HIDDEN = 256
SLC_LEN = 64
SLC_SHIFT = SLC_LEN.bit_length() - 1
SLC_TOPN = 16
WINDOW = 512
FORCED = 1e4
NEG = -1e30
ATT_SCALE = HEAD_DIM ** -0.5
LOG2E = 1.4426950408889634

VMEM_LIMIT = 56 * 1024 * 1024

GLA_V, GLA_R, GLA_Q, GLA_K, GLA_XQ, GLA_XG, GLA_GLR, GLA_NP = 0, 1536, 3072, 4096, 5120, 5632, 6144, 6400
(NSA_Q, NSA_KC, NSA_VC, NSA_KS, NSA_VS, NSA_KW, NSA_VW, NSA_R, NSA_XQ, NSA_XG, NSA_GL, NSA_NP) = (
    0, 1536, 1792, 2048, 2304, 2560, 2816, 3072, 4608, 5120, 5632, 6144)


def _params(sem):
    return pltpu.CompilerParams(dimension_semantics=sem, vmem_limit_bytes=VMEM_LIMIT)


def _silu(x):
    return x * jax.nn.sigmoid(x)


def _dot_nt(a, b):
    return lax.dot_general(a, b, (((1,), (1,)), ((), ())), preferred_element_type=F32)


def _dot_tn(a, b):
    return lax.dot_general(a, b, (((0,), (0,)), ((), ())), preferred_element_type=F32)


def _rep(x, width):
    return x[:, :width] if width < LANES else jnp.concatenate([x] * (width // LANES), axis=1)


def _split_dot(x, w):
    hi = x.astype(BF16)
    lo = (x - hi.astype(F32)).astype(BF16)
    return (jnp.dot(hi, w, preferred_element_type=F32) + jnp.dot(lo, w, preferred_element_type=F32))


def _norm_proj_body(x_ref, g_ref, w_ref, o_ref, xn_ref):
    @pl.when(pl.program_id(1) == 0)
    def _():
        x = x_ref[...]
        r = lax.rsqrt(jnp.mean(x * x, axis=-1, keepdims=True) + EPS)
        xn_ref[...] = ((x * r) * g_ref[...]).astype(BF16)

    o_ref[...] = jnp.dot(xn_ref[...], w_ref[...], preferred_element_type=F32).astype(o_ref.dtype)


def _norm_proj(x, g, w, tm, tn):
    rows, d = x.shape
    n = w.shape[1]
    return pl.pallas_call(
        _norm_proj_body,
        grid=(rows // tm, n // tn),
        in_specs=[pl.BlockSpec((tm, d), lambda i, j: (i, 0)),
                  pl.BlockSpec((1, d), lambda i, j: (0, 0)),
                  pl.BlockSpec((d, tn), lambda i, j: (0, j))],
        out_specs=pl.BlockSpec((tm, tn), lambda i, j: (i, j)),
        out_shape=jax.ShapeDtypeStruct((rows, n), BF16),
        scratch_shapes=[pltpu.VMEM((tm, d), BF16)],
        compiler_params=_params(("parallel", "arbitrary")),
        name="norm_proj",
    )(x, g.reshape(1, d), w)


def _mem_attn_body(xq_ref, xg_ref, kv_ref, o_ref):
    for h in range(X_HEADS):
        c = slice(h * HEAD_DIM, (h + 1) * HEAD_DIM)
        k = kv_ref[:, c]
        v = kv_ref[:, X_WIDTH + h * HEAD_DIM: X_WIDTH + (h + 1) * HEAD_DIM]
        s = _dot_nt(xq_ref[:, c], k) * ATT_SCALE
        e = jnp.exp(s - jnp.max(s, axis=-1, keepdims=True))
        p = e / jnp.sum(e, axis=-1, keepdims=True)
        o = jnp.dot(p.astype(BF16), v, preferred_element_type=F32)
        o_ref[:, c] = (o * _silu(xg_ref[:, c].astype(F32))).astype(o_ref.dtype)


def _mem_attn(proj, kv, xq_blk, xg_blk, tq):
    t = proj.shape[0]
    m = kv.shape[0]
    return pl.pallas_call(
        _mem_attn_body,
        grid=(t // tq,),
        in_specs=[pl.BlockSpec((tq, X_WIDTH), lambda i: (i, xq_blk)),
                  pl.BlockSpec((tq, X_WIDTH), lambda i: (i, xg_blk)),
                  pl.BlockSpec((m, 2 * X_WIDTH), lambda i: (0, 0))],
        out_specs=pl.BlockSpec((tq, X_WIDTH), lambda i: (i, 0)),
        out_shape=jax.ShapeDtypeStruct((t, X_WIDTH), BF16),
        compiler_params=_params(("parallel",)),
        name="mem_attn",
    )(proj, proj, kv)


def _out_proj_body(ys_ref, ym_ref, ws_ref, wm_ref, g_ref, h_ref, o_ref):
    y = (jnp.dot(ys_ref[...], ws_ref[...], preferred_element_type=F32)
         + jnp.dot(ym_ref[...], wm_ref[...], preferred_element_type=F32))
    r = lax.rsqrt(jnp.mean(y * y, axis=-1, keepdims=True) + EPS)
    o_ref[...] = h_ref[...] + (y * r) * g_ref[...]


def _out_proj(y_seq, y_mem, w_out, g_post, h, tm):
    t, d = h.shape
    ws = w_out[:BRANCH_WIDTH].astype(BF16)
    wm = w_out[BRANCH_WIDTH:].astype(BF16)
    return pl.pallas_call(
        _out_proj_body,
        grid=(t // tm,),
        in_specs=[pl.BlockSpec((tm, BRANCH_WIDTH), lambda i: (i, 0)),
                  pl.BlockSpec((tm, X_WIDTH), lambda i: (i, 0)),
                  pl.BlockSpec((BRANCH_WIDTH, d), lambda i: (0, 0)),
                  pl.BlockSpec((X_WIDTH, d), lambda i: (0, 0)),
                  pl.BlockSpec((1, d), lambda i: (0, 0)),
                  pl.BlockSpec((tm, d), lambda i: (i, 0))],
        out_specs=pl.BlockSpec((tm, d), lambda i: (i, 0)),
        out_shape=jax.ShapeDtypeStruct((t, d), F32),
        compiler_params=_params(("parallel",)),
        name="out_proj",
    )(y_seq, y_mem, ws, wm, g_post.reshape(1, d), h)


def _gla_body(q_ref, k_ref, v_ref, r_ref, glr_ref, wg_ref, bg_ref, gn_ref, tri_ref, y_ref,
              st_ref, b_sc, q_sc, k_sc, v_sc, o_sc, *, chunks):
    C = GLA_CHUNK

    @pl.when(pl.program_id(1) == 0)
    def _():
        st_ref[...] = jnp.zeros_like(st_ref)

    def chunk(c, carry):
        rows = pl.ds(pl.multiple_of(c * C, C), C)
        gp = jnp.dot(glr_ref[rows, :], wg_ref[...], preferred_element_type=F32) + bg_ref[...]
        logg = (jnp.minimum(gp, 0.0) - jnp.log1p(jnp.exp(-jnp.abs(gp)))) * (1.0 / GLA_TEMP)
        b = _split_dot_left(tri_ref[...], logg)
        q = q_ref[rows, :].astype(F32) * (GLA_DK ** -0.5)
        k = k_ref[rows, :].astype(F32)
        v = v_ref[rows, :]
        b_sc[...] = b
        q_sc[...] = q
        k_sc[...] = k
        v_sc[...] = v.astype(F32)
        st = st_ref[...]
        o_sc[...] = _dot_nt((q * jnp.exp(b)).astype(BF16), st.astype(BF16))

        SB = GLA_SUB
        for j in range(C // SB - 1):
            lo, hi = j * SB, (j + 1) * SB
            beta = b_sc[hi:hi + 1, :]
            qt = (q_sc[hi:, :] * jnp.exp(b_sc[hi:, :] - beta)).astype(BF16)
            kt = (k_sc[lo:hi, :] * jnp.exp(beta - b_sc[lo:hi, :])).astype(BF16)
            a = _dot_nt(qt, kt)
            vj = v_ref[pl.ds(pl.multiple_of(c * C + lo, SB), SB), :]
            o_sc[hi:, :] += jnp.dot(a.astype(BF16), vj, preferred_element_type=F32)

        sub8 = lax.broadcasted_iota(jnp.int32, (8, 1), 0)
        for blk in range(C // SB):
            base = blk * SB
            acc = [jnp.zeros((8, GLA_DV), F32) for _ in range(SB // 8)]
            for s in range(SB):
                row = base + s
                bs = b_sc[row:row + 1, :]
                ks = k_sc[row:row + 1, :]
                vs = v_sc[row:row + 1, :]
                for tix in range(s // 8, SB // 8):
                    r0 = base + tix * 8
                    e = jnp.exp(b_sc[r0:r0 + 8, :] - bs)
                    if tix == s // 8:
                        e = jnp.where(sub8 >= s % 8, e, 0.0)
                    a = jnp.sum(q_sc[r0:r0 + 8, :] * ks * e, axis=-1, keepdims=True)
                    acc[tix] = acc[tix] + a * vs
            for tix in range(SB // 8):
                r0 = base + tix * 8
                o_sc[r0:r0 + 8, :] += acc[tix]

        bl = b[C - 1:C, :]
        kd = (k * jnp.exp(bl - b)).astype(BF16)
        st_ref[...] = jnp.exp(bl) * st + _dot_tn(v, kd)

        o = o_sc[...]
        on = (o * lax.rsqrt(jnp.mean(o * o, axis=-1, keepdims=True) + EPS)) * gn_ref[...]
        y_ref[rows, :] = (on * _silu(r_ref[rows, :].astype(F32))).astype(y_ref.dtype)
        return carry

    lax.fori_loop(0, chunks, chunk, 0)


def _split_dot_left(w, x):
    hi = x.astype(BF16)
    lo = (x - hi.astype(F32)).astype(BF16)
    return (jnp.dot(w, hi, preferred_element_type=F32) + jnp.dot(w, lo, preferred_element_type=F32))


def _gla_scan(proj, wg, bg, gnorm, tb):
    t = proj.shape[0]
    C = GLA_CHUNK
    tri = jnp.asarray(np.tril(np.ones((C, C), np.float32)), BF16)
    qb, kb, vb, rb, gb = GLA_Q // GLA_DKP, GLA_K // GLA_DKP, GLA_V // GLA_DV, GLA_R // GLA_DV, GLA_GLR // GLA_RANKP
    return pl.pallas_call(
        functools.partial(_gla_body, chunks=tb // C),
        grid=(GLA_HEADS, t // tb),
        in_specs=[pl.BlockSpec((tb, GLA_DKP), lambda h, n: (n, qb + h)),
                  pl.BlockSpec((tb, GLA_DKP), lambda h, n: (n, kb + h)),
                  pl.BlockSpec((tb, GLA_DV), lambda h, n: (n, vb + h)),
                  pl.BlockSpec((tb, GLA_DV), lambda h, n: (n, rb + h)),
                  pl.BlockSpec((tb, GLA_RANKP), lambda h, n: (n, gb)),
                  pl.BlockSpec((None, GLA_RANKP, GLA_DKP), lambda h, n: (h, 0, 0)),
                  pl.BlockSpec((None, 1, GLA_DKP), lambda h, n: (h, 0, 0)),
                  pl.BlockSpec((1, GLA_DV), lambda h, n: (0, 0)),
                  pl.BlockSpec((C, C), lambda h, n: (0, 0))],
        out_specs=pl.BlockSpec((tb, GLA_DV), lambda h, n: (n, h)),
        out_shape=jax.ShapeDtypeStruct((t, BRANCH_WIDTH), BF16),
        scratch_shapes=[pltpu.VMEM((GLA_DV, GLA_DKP), F32),
                        pltpu.VMEM((C, GLA_DKP), F32),
                        pltpu.VMEM((C, GLA_DKP), F32),
                        pltpu.VMEM((C, GLA_DKP), F32),
                        pltpu.VMEM((C, GLA_DV), F32),
                        pltpu.VMEM((C, GLA_DV), F32)],
        compiler_params=_params(("parallel", "arbitrary")),
        name="gla_scan",
    )(proj, proj, proj, proj, proj, wg, bg, gnorm.reshape(1, GLA_DV), tri)


def _pad_heads(w, heads, width, padded):
    d = w.shape[0]
    return jnp.pad(w.reshape(d, heads, width), ((0, 0), (0, 0), (0, padded - width))).reshape(d, heads * padded)


def _gla_layer_seq(xn_proj_fn, w_in, w_gate_lr, b_gate, g_norm):
    sizes = np.cumsum([GLA_HEADS * GLA_DK, GLA_HEADS * GLA_DK, BRANCH_WIDTH, GLA_RANK, BRANCH_WIDTH, X_WIDTH])
    wq, wk, wv, wglr, wr, wxq, wxg = jnp.split(w_in, sizes.tolist(), axis=1)
    w_cat = jnp.concatenate([
        wv, wr, _pad_heads(wq, GLA_HEADS, GLA_DK, GLA_DKP), _pad_heads(wk, GLA_HEADS, GLA_DK, GLA_DKP), wxq, wxg,
        jnp.pad(wglr, ((0, 0), (0, GLA_NP - GLA_GLR - GLA_RANK)))], axis=1).astype(BF16)
    proj = xn_proj_fn(w_cat, 1280)
    wg = jnp.pad(w_gate_lr.reshape(GLA_RANK, GLA_HEADS, GLA_DK).transpose(1, 0, 2),
                 ((0, 0), (0, GLA_RANKP - GLA_RANK), (0, GLA_DKP - GLA_DK))).astype(BF16)
    bg = jnp.pad(b_gate.reshape(GLA_HEADS, 1, GLA_DK), ((0, 0), (0, 0), (0, GLA_DKP - GLA_DK)))
    y_seq = _gla_scan(proj, wg, bg, g_norm, 512)
    return proj, y_seq, GLA_XQ // X_WIDTH, GLA_XG // X_WIDTH


def _cmp_mlp_body(x_ref, pe_ref, w1_ref, w2_ref, o_ref):
    x = (x_ref[...].astype(F32) + pe_ref[...]).astype(BF16)
    hid = _silu(jnp.dot(x, w1_ref[...], preferred_element_type=F32))
    o_ref[...] = jnp.dot(hid.astype(BF16), w2_ref[...], preferred_element_type=F32).astype(o_ref.dtype)


def _cmp_mlp(blocks, pe, w1, w2, tr):
    _, rows, width = blocks.shape
    return pl.pallas_call(
        _cmp_mlp_body,
        grid=(2, rows // tr),
        in_specs=[pl.BlockSpec((None, tr, width), lambda a, i: (a, i, 0)),
                  pl.BlockSpec((None, 1, width), lambda a, i: (a, 0, 0)),
                  pl.BlockSpec((None, width, CMP_HIDDEN), lambda a, i: (a, 0, 0)),
                  pl.BlockSpec((None, CMP_HIDDEN, HEAD_DIM), lambda a, i: (a, 0, 0))],
        out_specs=pl.BlockSpec((None, tr, HEAD_DIM), lambda a, i: (a, i, 0)),
        out_shape=jax.ShapeDtypeStruct((2, rows, HEAD_DIM), BF16),
        compiler_params=_params(("parallel", "parallel")),
        name="cmp_mlp",
    )(blocks, pe, w1, w2)


def _cmp_attn_body(q_ref, kct_ref, vca_ref, ov_ref, o_ref, imp_ref, qs_sc, imp_sc, *, tq, nb, widths):
    i = pl.program_id(0)
    n_tiles = pl.num_programs(0)
    for hh in range(NSA_HEADS):
        c = slice(hh * HEAD_DIM, (hh + 1) * HEAD_DIM)
        qs_sc[hh] = (q_ref[:, c].astype(F32) * (ATT_SCALE * LOG2E)).astype(BF16)

    def run(w):
        t = i * tq + lax.broadcasted_iota(jnp.int32, (tq, 1), 0)
        n = lax.broadcasted_iota(jnp.int32, (1, w), 1)
        bias = jnp.where(n * CMP_STRIDE + (CMP_LEN - 1) <= t, 0.0, NEG)
        has_key = t >= CMP_LEN - 1
        R = CMP_ROWS
        for g in range(NSA_GROUPS):
            for h in range(NSA_HPG):
                hh = g * NSA_HPG + h
                c = slice(hh * HEAD_DIM, (hh + 1) * HEAD_DIM)
                for r in range(tq // R):
                    rows = slice(r * R, (r + 1) * R)
                    s = jnp.dot(qs_sc[hh, rows, :], kct_ref[g, :, :w], preferred_element_type=F32) + bias[rows, :]
                    e = jnp.exp2(s - jnp.max(s, axis=-1, keepdims=True))
                    pv = jnp.dot(e.astype(BF16), vca_ref[g, :w, :], preferred_element_type=F32)
                    inv = jnp.where(has_key[rows, :], 1.0 / pv[:, HEAD_DIM:], 0.0)
                    o_ref[rows, c] = (pv[:, :HEAD_DIM] * inv).astype(o_ref.dtype)
                    p = e * _rep(inv, w)
                    if h == 0:
                        imp_sc[rows, :w] = p
                    else:
                        imp_sc[rows, :w] += p
            imp_ref[:, g * nb:(g + 1) * nb] = _split_dot(imp_sc[:, :w], ov_ref[:w, :])

    per_class = n_tiles // len(widths)
    for cls, w in enumerate(widths):
        pl.when(i // per_class == cls)(functools.partial(run, w))


def _cmp_attn(proj, kct, vca, overlap, tq):
    t = proj.shape[0]
    nc, nb = overlap.shape
    classes = 4
    widths = tuple(nc * (c + 1) // classes for c in range(classes))
    assert (t // tq) % classes == 0 and all(w % min(LANES, nc // classes) == 0 for w in widths)
    return pl.pallas_call(
        functools.partial(_cmp_attn_body, tq=tq, nb=nb, widths=widths),
        grid=(t // tq,),
        in_specs=[pl.BlockSpec((tq, BRANCH_WIDTH), lambda i: (i, NSA_Q // BRANCH_WIDTH)),
                  pl.BlockSpec((NSA_GROUPS, HEAD_DIM, nc), lambda i: (0, 0, 0)),
                  pl.BlockSpec((NSA_GROUPS, nc, 2 * HEAD_DIM), lambda i: (0, 0, 0)),
                  pl.BlockSpec((nc, nb), lambda i: (0, 0))],
        out_specs=[pl.BlockSpec((tq, BRANCH_WIDTH), lambda i: (i, 0)),
                   pl.BlockSpec((tq, NSA_GROUPS * nb), lambda i: (i, 0))],
        out_shape=[jax.ShapeDtypeStruct((t, BRANCH_WIDTH), BF16),
                   jax.ShapeDtypeStruct((t, NSA_GROUPS * nb), F32)],
        scratch_shapes=[pltpu.VMEM((NSA_HEADS, tq, HEAD_DIM), BF16),
                        pltpu.VMEM((tq, nc), F32)],
        compiler_params=_params(("parallel",)),
        name="cmp_attn",
    )(proj, kct, vca, overlap)


def _topk_body(imp_ref, sel_ref, x_sc, *, tr, nb):
    t = pl.program_id(0) * tr + lax.broadcasted_iota(jnp.int32, (tr, 1), 0)
    jb = lax.broadcasted_iota(jnp.int32, (1, nb), 1)
    cur = jnp.right_shift(t, SLC_SHIFT)
    x = imp_ref[...]
    x = jnp.where((jb == 0) | (jb == cur) | (jb == cur - 1), FORCED, x)
    x_sc[...] = jnp.where(jb > cur, -FORCED, x)
    R = TOPK_ROWS
    jbf = jb.astype(F32)

    def pick(_, carry):
        for r in range(tr // R):
            rows = slice(r * R, (r + 1) * R)
            x = x_sc[rows, :]
            m = jnp.max(x, axis=-1, keepdims=True)
            idx = jnp.min(jnp.where(x == m, jbf, float(nb)), axis=-1, keepdims=True)
            x_sc[rows, :] = jnp.where(jbf == idx, PICKED, x)
        return carry

    lax.fori_loop(0, SLC_TOPN, pick, 0)
    sel_ref[...] = jnp.where(x_sc[...] == PICKED, 0.0, NEG).astype(sel_ref.dtype)


def _topk_select(imp, tr):
    t = imp.shape[0]
    nb = imp.shape[1] // NSA_GROUPS
    return pl.pallas_call(
        functools.partial(_topk_body, tr=tr, nb=nb),
        grid=(t // tr, NSA_GROUPS),
        in_specs=[pl.BlockSpec((tr, nb), lambda i, g: (i, g))],
        out_specs=pl.BlockSpec((tr, nb), lambda i, g: (i, g)),
        out_shape=jax.ShapeDtypeStruct(imp.shape, BF16),
        scratch_shapes=[pltpu.VMEM((tr, nb), F32)],
        compiler_params=_params(("parallel", "parallel")),
        name="topk_select",
    )(imp)


def _win_attn_body(q_ref, kp_ref, kc_ref, vp_ref, vc_ref, o_ref, qs_sc, *, tq):
    i = pl.program_id(1)
    R = WIN_ROWS
    for h in range(NSA_HPG):
        qs_sc[h] = (q_ref[:, h * HEAD_DIM:(h + 1) * HEAD_DIM].astype(F32) * (ATT_SCALE * LOG2E)).astype(BF16)
    col = lax.broadcasted_iota(jnp.int32, (1, R), 1)
    row = lax.broadcasted_iota(jnp.int32, (R, 1), 0)
    upto_query = jnp.where(col <= row, 0.0, NEG)
    inside_window = jnp.where(col > row, 0.0, NEG)

    def run(with_prev):
        for h in range(NSA_HPG):
            for r in range(tq // R):
                rows = slice(r * R, (r + 1) * R)
                n_cur = (r + 1) * R
                sc = jnp.dot(qs_sc[h, rows, :], kc_ref[:, :n_cur], preferred_element_type=F32)
                parts = ([sc[:, :n_cur - R]] if r > 0 else []) + [sc[:, n_cur - R:] + upto_query]
                if with_prev:
                    n_prev = tq - r * R
                    sp = jnp.dot(qs_sc[h, rows, :], kp_ref[:, r * R:], preferred_element_type=F32)
                    parts = [sp[:, :R] + inside_window] + ([sp[:, R:]] if n_prev > R else []) + parts
                s = parts[0] if len(parts) == 1 else jnp.concatenate(parts, axis=1)
                e = jnp.exp2(s - jnp.max(s, axis=-1, keepdims=True)).astype(BF16)
                if with_prev:
                    pv = (jnp.dot(e[:, :n_prev], vp_ref[r * R:, :], preferred_element_type=F32)
                          + jnp.dot(e[:, n_prev:], vc_ref[:n_cur, :], preferred_element_type=F32))
                else:
                    pv = jnp.dot(e, vc_ref[:n_cur, :], preferred_element_type=F32)
                o_ref[rows, h * HEAD_DIM:(h + 1) * HEAD_DIM] = (pv[:, :HEAD_DIM] / pv[:, HEAD_DIM:]).astype(o_ref.dtype)

    pl.when(i == 0)(functools.partial(run, False))
    pl.when(i > 0)(functools.partial(run, True))


def _win_attn(proj, kt, va, tq):
    t = proj.shape[0]
    assert tq == WINDOW
    prev = lambda i: jnp.maximum(i - 1, 0)
    return pl.pallas_call(
        functools.partial(_win_attn_body, tq=tq),
        grid=(NSA_GROUPS, t // tq),
        in_specs=[pl.BlockSpec((tq, NSA_GW), lambda g, i: (i, g)),
                  pl.BlockSpec((None, HEAD_DIM, tq), lambda g, i: (g, 0, prev(i))),
                  pl.BlockSpec((None, HEAD_DIM, tq), lambda g, i: (g, 0, i)),
                  pl.BlockSpec((None, tq, 2 * HEAD_DIM), lambda g, i: (g, prev(i), 0)),
                  pl.BlockSpec((None, tq, 2 * HEAD_DIM), lambda g, i: (g, i, 0))],
        out_specs=pl.BlockSpec((tq, NSA_GW), lambda g, i: (i, g)),
        out_shape=jax.ShapeDtypeStruct((t, BRANCH_WIDTH), BF16),
        scratch_shapes=[pltpu.VMEM((NSA_HPG, tq, HEAD_DIM), BF16)],
        compiler_params=_params(("parallel", "parallel")),
        name="win_attn",
    )(proj, kt, kt, va, va)


def _slc_attn_body(it_ref, jt_ref, q_ref, k_ref, v_ref, sel_ref, ocmp_ref, owin_ref, gl_ref, r_ref, y_ref,
                   qa_sc, m_sc, acc_sc, *, tq, tk):
    step = pl.program_id(1)
    i = it_ref[step]
    j = jt_ref[step]

    @pl.when(j == 0)
    def _():
        for h in range(NSA_HPG):
            cs = slice(h * HEAD_DIM, (h + 1) * HEAD_DIM)
            qa_sc[h, :, :HEAD_DIM] = (q_ref[:, cs].astype(F32) * (ATT_SCALE * LOG2E)).astype(BF16)
        m_sc[...] = jnp.full_like(m_sc, NEG)
        acc_sc[...] = jnp.zeros_like(acc_sc)

    sel = sel_ref[...]
    for h in range(NSA_HPG):
        qa_sc[h, :, HEAD_DIM:] = sel

    def accumulate(diag):
        R = SLC_ROWS
        if diag:
            tri = jnp.where(lax.broadcasted_iota(jnp.int32, (1, R), 1) <= lax.broadcasted_iota(jnp.int32, (R, 1), 0),
                            0.0, NEG)
        for h in range(NSA_HPG):
            for r in range(tq // R):
                rows = slice(r * R, (r + 1) * R)
                ncols = (r + 1) * R if diag else tk
                s = jnp.dot(qa_sc[h, rows, :], k_ref[:, :ncols], preferred_element_type=F32)
                if diag:
                    tail = s[:, ncols - R:] + tri
                    s = tail if r == 0 else jnp.concatenate([s[:, :ncols - R], tail], axis=1)
                m_old = m_sc[h, rows, :]
                m_new = jnp.maximum(m_old, jnp.max(s, axis=-1, keepdims=True))
                p = jnp.exp2(s - _rep(m_new, ncols))
                alpha = jnp.exp2(m_old - m_new)
                acc_sc[h, rows, :] = (_rep(alpha, 2 * HEAD_DIM) * acc_sc[h, rows, :]
                                      + jnp.dot(p.astype(BF16), v_ref[:ncols, :], preferred_element_type=F32))
                m_sc[h, rows, :] = m_new

    @pl.when(j < i)
    def _():
        accumulate(False)

    @pl.when(j == i)
    def _():
        accumulate(True)
        gates = jax.nn.sigmoid(gl_ref[...].astype(F32))
        for h in range(NSA_HPG):
            cs = slice(h * HEAD_DIM, (h + 1) * HEAD_DIM)
            o_slc = acc_sc[h, :, :HEAD_DIM] / acc_sc[h, :, HEAD_DIM:]
            g0 = gates[:, h * N_BRANCH + 0:h * N_BRANCH + 1]
            g1 = gates[:, h * N_BRANCH + 1:h * N_BRANCH + 2]
            g2 = gates[:, h * N_BRANCH + 2:h * N_BRANCH + 3]
            o = g0 * ocmp_ref[:, cs].astype(F32) + g1 * o_slc + g2 * owin_ref[:, cs].astype(F32)
            y_ref[:, cs] = (o * _silu(r_ref[:, cs].astype(F32))).astype(y_ref.dtype)


def _slc_attn(proj, ka, va, sel, o_cmp, o_win, tile):
    t = proj.shape[0]
    nb = sel.shape[1] // NSA_GROUPS
    lw = ka.shape[1] - HEAD_DIM
    ni = t // tile
    it = np.concatenate([np.full(i + 1, i) for i in range(ni)]).astype(np.int32)
    jt = np.concatenate([np.arange(i + 1) for i in range(ni)]).astype(np.int32)
    blocks_per_tile = tile // SLC_LEN
    sel_idx = lambda g, s, it, jt: (it[s], g * (nb // lw) + (jt[s] * blocks_per_tile) // lw)
    row = lambda g, s, it, jt: (it[s], g)
    grid_spec = pltpu.PrefetchScalarGridSpec(
        num_scalar_prefetch=2,
        grid=(NSA_GROUPS, len(it)),
        in_specs=[pl.BlockSpec((tile, NSA_GW), row),
                  pl.BlockSpec((None, HEAD_DIM + lw, tile), lambda g, s, it, jt: (g, 0, jt[s])),
                  pl.BlockSpec((None, tile, 2 * HEAD_DIM), lambda g, s, it, jt: (g, jt[s], 0)),
                  pl.BlockSpec((tile, lw), sel_idx),
                  pl.BlockSpec((tile, NSA_GW), row),
                  pl.BlockSpec((tile, NSA_GW), row),
                  pl.BlockSpec((tile, HEAD_DIM), lambda g, s, it, jt: (it[s], NSA_GL // HEAD_DIM + g)),
                  pl.BlockSpec((tile, NSA_GW), lambda g, s, it, jt: (it[s], NSA_R // NSA_GW + g))],
        out_specs=pl.BlockSpec((tile, NSA_GW), row),
        scratch_shapes=[pltpu.VMEM((NSA_HPG, tile, HEAD_DIM + lw), BF16),
                        pltpu.VMEM((NSA_HPG, tile, HEAD_DIM), F32),
                        pltpu.VMEM((NSA_HPG, tile, 2 * HEAD_DIM), F32)])
    return pl.pallas_call(
        functools.partial(_slc_attn_body, tq=tile, tk=tile),
        grid_spec=grid_spec,
        out_shape=jax.ShapeDtypeStruct((t, BRANCH_WIDTH), BF16),
        compiler_params=_params(("parallel", "arbitrary")),
        name="slc_attn",
    )(jnp.asarray(it), jnp.asarray(jt), proj, ka, va, sel, o_cmp, o_win, proj, proj)


def _nsa_layer_seq(xn_proj_fn, w_in, pe_k, pe_v, wk1, wk2, wv1, wv2):
    t_sizes = [BRANCH_WIDTH] + [NSA_GROUPS * HEAD_DIM] * 6 + [NSA_HEADS * N_BRANCH, BRANCH_WIDTH, X_WIDTH]
    parts = jnp.split(w_in, np.cumsum(t_sizes).tolist(), axis=1)
    wq, kv6, wgl, wr, wxq, wxg = parts[0], parts[1:7], parts[7], parts[8], parts[9], parts[10]
    wgl = _pad_heads(wgl, NSA_GROUPS, NSA_HPG * N_BRANCH, HEAD_DIM)
    w_cat = jnp.concatenate([wq] + list(kv6) + [wr, wxq, wxg, wgl], axis=1)
    w_cat = jnp.pad(w_cat, ((0, 0), (0, NSA_NP - w_cat.shape[1]))).astype(BF16)
    proj = xn_proj_fn(w_cat, 1024)
    t = proj.shape[0]

    half = CMP_STRIDE
    nc = t // half
    a = proj[:, NSA_KC:NSA_KS].reshape(nc, half, 2, NSA_GROUPS, HEAD_DIM)
    a = a.transpose(2, 3, 0, 1, 4).reshape(2, NSA_GROUPS, nc, half * HEAD_DIM)
    blocks = jnp.concatenate([a, jnp.roll(a, -1, axis=2)], axis=-1).reshape(2, NSA_GROUPS * nc, CMP_LEN * HEAD_DIM)
    pe = jnp.stack([pe_k, pe_v]).reshape(2, 1, CMP_LEN * HEAD_DIM)
    kv_cmp = _cmp_mlp(blocks, pe, jnp.stack([wk1, wv1]).astype(BF16), jnp.stack([wk2, wv2]).astype(BF16),
                      min(512, NSA_GROUPS * nc))
    kv_cmp = kv_cmp.reshape(2, NSA_GROUPS, nc, HEAD_DIM)

    nb = t // SLC_LEN
    cstart = np.arange(nc) * CMP_STRIDE
    sstart = np.arange(nb) * SLC_LEN
    overlap = ((cstart[:, None] < sstart[None, :] + SLC_LEN) & (cstart[:, None] + CMP_LEN > sstart[None, :]))
    overlap[nc - 1] = False
    kct = kv_cmp[0].transpose(0, 2, 1)
    vca = jnp.concatenate([kv_cmp[1], jnp.ones_like(kv_cmp[1])], axis=-1)
    o_cmp, imp = _cmp_attn(proj, kct, vca, jnp.asarray(overlap.astype(np.float32), BF16), 256)
    sel = _topk_select(imp, min(1024, t))
    kw = proj[:, NSA_KW:NSA_VW].reshape(t, NSA_GROUPS, HEAD_DIM).transpose(1, 2, 0)
    vw = proj[:, NSA_VW:NSA_R].reshape(t, NSA_GROUPS, HEAD_DIM).transpose(1, 0, 2)
    o_win = _win_attn(proj, kw, jnp.concatenate([vw, jnp.ones_like(vw)], axis=-1), WINDOW)
    lw = min(HEAD_DIM, nb)
    onehot = (np.arange(t)[:, None] // SLC_LEN % lw == np.arange(lw)[None, :]).astype(np.float32)
    ks = proj[:, NSA_KS:NSA_VS].reshape(t, NSA_GROUPS, HEAD_DIM).transpose(1, 0, 2)
    vs = proj[:, NSA_VS:NSA_KW].reshape(t, NSA_GROUPS, HEAD_DIM).transpose(1, 0, 2)
    ka = jnp.concatenate([ks, jnp.broadcast_to(jnp.asarray(onehot, BF16), (NSA_GROUPS, t, lw))], axis=-1)
    ka = ka.transpose(0, 2, 1)
    va = jnp.concatenate([vs, jnp.ones((NSA_GROUPS, t, HEAD_DIM), BF16)], axis=-1)
    y_seq = _slc_attn(proj, ka, va, sel, o_cmp, o_win, 512)
    return proj, y_seq, NSA_XQ // X_WIDTH, NSA_XG // X_WIDTH


def kernel(x, mem, ln_pre, ln_post, ln_mem, w_mem_kv, gla_w_in, gla_w_gate_lr, gla_b_gate, gla_norm, gla_w_out,
           nsa_w_in, nsa_pe_k, nsa_pe_v, nsa_wk1, nsa_wk2, nsa_wv1, nsa_wv2, nsa_w_out):
    batch, t, d = x.shape
    assert batch == 1 and d == D_MODEL and t % 1024 == 0
    h = x.reshape(t, d)
    mem2 = mem.reshape(mem.shape[1], d)
    depth = ln_pre.shape[0]
    for i in range(depth):
        a = i // 2
        xn_proj_fn = lambda w, tn, i=i, h=h: _norm_proj(h, ln_pre[i], w, 1024, tn)
        kv = _norm_proj(mem2, ln_mem[i], w_mem_kv[i].astype(BF16), mem2.shape[0], 2 * X_WIDTH)
        if i % 2 == 0:
            proj, y_seq, xq_blk, xg_blk = _gla_layer_seq(xn_proj_fn, gla_w_in[a], gla_w_gate_lr[a], gla_b_gate[a],
                                                         gla_norm[a])
            w_out = gla_w_out[a]
        else:
            proj, y_seq, xq_blk, xg_blk = _nsa_layer_seq(xn_proj_fn, nsa_w_in[a], nsa_pe_k[a], nsa_pe_v[a],
                                                         nsa_wk1[a], nsa_wk2[a], nsa_wv1[a], nsa_wv2[a])
            w_out = nsa_w_out[a]
        y_mem = _mem_attn(proj, kv, xq_blk, xg_blk, 512)
        h = _out_proj(y_seq, y_mem, w_out, ln_post[i], h, 256)
    return h.reshape(batch, t, d)
```

```python
import functools

import numpy as np
import jax
import jax.numpy as jnp
from jax import lax
from jax.experimental import pallas as pl
from jax.experimental.pallas import tpu as pltpu

F32 = jnp.float32
BF16 = jnp.bfloat16

D_MODEL = 2048
HEAD_DIM = 128
EPS = 1e-6
X_HEADS = 4
X_WIDTH = X_HEADS * HEAD_DIM
BRANCH_WIDTH = D_MODEL - X_WIDTH
GLA_HEADS = 4
GLA_DV = BRANCH_WIDTH // GLA_HEADS
GLA_DK = GLA_DV // 2
GLA_DKP = 256
GLA_RANK = 16
GLA_RANKP = 128
GLA_TEMP = 16.0
GLA_CHUNK = 64
GLA_SUB = 16
NSA_HEADS = BRANCH_WIDTH // HEAD_DIM
NSA_GROUPS = 2
NSA_HPG = NSA_HEADS // NSA_GROUPS
NSA_GW = NSA_HPG * HEAD_DIM
N_BRANCH = 3
CMP_LEN = 32
CMP_STRIDE = 16
CMP_HIDDEN = 256
SLC_LEN = 64
SLC_SHIFT = SLC_LEN.bit_length() - 1
SLC_TOPN = 16
WINDOW = 512
FORCED = 1e4
NEG = -1e30
PICKED = -3e38
ATT_SCALE = HEAD_DIM ** -0.5
LOG2E = 1.4426950408889634

LANES = 128
VMEM_LIMIT = 56 * 1024 * 1024

PROJ_TM = 1024
GLA_TN, NSA_TN = 1280, 768
GLA_TB = 512
GLA_HEADS_PER_STEP = 2
MEM_TQ = 512
OUT_TM = 256
CMP_MLP_TR = 512
CMP_TQ = 256
TOPK_TR = 1024
SLC_TILE = 512
SLC_ROWS = 256
CMP_ROWS = 128
WIN_ROWS = 128
TOPK_LANES = 128

GLA_V, GLA_R, GLA_Q, GLA_K, GLA_XQ, GLA_XG, GLA_GLR, GLA_NP = 0, 1536, 3072, 4096, 5120, 5632, 6144, 6400
(NSA_Q, NSA_R, NSA_KC, NSA_VC, NSA_VS, NSA_VW, NSA_XQ, NSA_XG, NSA_GL, NSA_NP) = (
    0, 1536, 3072, 3328, 3584, 3840, 4096, 4608, 5120, 5376)
NSA_KT_SLC, NSA_KT_WIN = 0, NSA_GROUPS


def _params(sem):
    return pltpu.CompilerParams(dimension_semantics=sem, vmem_limit_bytes=VMEM_LIMIT)


def _silu(x):
    return x * jax.nn.sigmoid(x)


def _dot_nt(a, b):
    return lax.dot_general(a, b, (((1,), (1,)), ((), ())), preferred_element_type=F32)


def _dot_tn(a, b):
    return lax.dot_general(a, b, (((0,), (0,)), ((), ())), preferred_element_type=F32)


def _rep(x, width):
    return x[:, :width] if width < LANES else jnp.concatenate([x] * (width // LANES), axis=1)


def _split_dot_nt(w, x):
    hi = x.astype(BF16)
    lo = (x - hi.astype(F32)).astype(BF16)
    return _dot_nt(w, hi) + _dot_nt(w, lo)


def _split_dot_left(w, x):
    hi = x.astype(BF16)
    lo = (x - hi.astype(F32)).astype(BF16)
    return (jnp.dot(w, hi, preferred_element_type=F32) + jnp.dot(w, lo, preferred_element_type=F32))


def _norm_proj_body(x_ref, g_ref, w_ref, *rest, transposed):
    if transposed:
        wt_ref, o_ref, ot_ref, xn_ref = rest
    else:
        o_ref, xn_ref = rest

    @pl.when(pl.program_id(1) == 0)
    def _():
        x = x_ref[...]
        r = lax.rsqrt(jnp.mean(x * x, axis=-1, keepdims=True) + EPS)
        xn_ref[...] = ((x * r) * g_ref[...]).astype(BF16)
        if transposed:
            ot_ref[...] = _dot_nt(wt_ref[...], xn_ref[...]).astype(ot_ref.dtype)

    o_ref[...] = jnp.dot(xn_ref[...], w_ref[...], preferred_element_type=F32).astype(o_ref.dtype)


def _norm_proj(x, g, w, tm, tn, wt=None):
    rows, d = x.shape
    n = w.shape[1]
    in_specs = [pl.BlockSpec((tm, d), lambda i, j: (i, 0)),
                pl.BlockSpec((1, d), lambda i, j: (0, 0)),
                pl.BlockSpec((d, tn), lambda i, j: (0, j))]
    out_specs = pl.BlockSpec((tm, tn), lambda i, j: (i, j))
    out_shape = jax.ShapeDtypeStruct((rows, n), BF16)
    args = (x, g.reshape(1, d), w)
    if wt is not None:
        nt = wt.shape[0]
        in_specs.append(pl.BlockSpec((nt, d), lambda i, j: (0, 0)))
        out_specs = [out_specs, pl.BlockSpec((nt, tm), lambda i, j: (0, i))]
        out_shape = [out_shape, jax.ShapeDtypeStruct((nt, rows), BF16)]
        args = args + (wt,)
    return pl.pallas_call(
        functools.partial(_norm_proj_body, transposed=wt is not None),
        grid=(rows // tm, n // tn),
        in_specs=in_specs,
        out_specs=out_specs,
        out_shape=out_shape,
        scratch_shapes=[pltpu.VMEM((tm, d), BF16)],
        compiler_params=_params(("parallel", "arbitrary")),
        name="norm_proj",
    )(*args)


def _mem_attn_body(xq_ref, xg_ref, kv_ref, o_ref):
    for h in range(X_HEADS):
        c = slice(h * HEAD_DIM, (h + 1) * HEAD_DIM)
        k = kv_ref[:, c]
        v = kv_ref[:, X_WIDTH + h * HEAD_DIM: X_WIDTH + (h + 1) * HEAD_DIM]
        s = _dot_nt(xq_ref[:, c], k) * ATT_SCALE
        e = jnp.exp(s - jnp.max(s, axis=-1, keepdims=True))
        p = e / jnp.sum(e, axis=-1, keepdims=True)
        o = jnp.dot(p.astype(BF16), v, preferred_element_type=F32)
        o_ref[:, c] = (o * _silu(xg_ref[:, c].astype(F32))).astype(o_ref.dtype)


def _mem_attn(proj, kv, xq_blk, xg_blk, tq):
    t = proj.shape[0]
    m = kv.shape[0]
    return pl.pallas_call(
        _mem_attn_body,
        grid=(t // tq,),
        in_specs=[pl.BlockSpec((tq, X_WIDTH), lambda i: (i, xq_blk)),
                  pl.BlockSpec((tq, X_WIDTH), lambda i: (i, xg_blk)),
                  pl.BlockSpec((m, 2 * X_WIDTH), lambda i: (0, 0))],
        out_specs=pl.BlockSpec((tq, X_WIDTH), lambda i: (i, 0)),
        out_shape=jax.ShapeDtypeStruct((t, X_WIDTH), BF16),
        compiler_params=_params(("parallel",)),
        name="mem_attn",
    )(proj, proj, kv)


def _out_proj_body(ys_ref, ym_ref, ws_ref, wm_ref, g_ref, h_ref, o_ref):
    y = (jnp.dot(ys_ref[...], ws_ref[...], preferred_element_type=F32)
         + jnp.dot(ym_ref[...], wm_ref[...], preferred_element_type=F32))
    r = lax.rsqrt(jnp.mean(y * y, axis=-1, keepdims=True) + EPS)
    o_ref[...] = h_ref[...] + (y * r) * g_ref[...]


def _out_proj(y_seq, y_mem, w_out, g_post, h, tm):
    t, d = h.shape
    ws = w_out[:BRANCH_WIDTH].astype(BF16)
    wm = w_out[BRANCH_WIDTH:].astype(BF16)
    return pl.pallas_call(
        _out_proj_body,
        grid=(t // tm,),
        in_specs=[pl.BlockSpec((tm, BRANCH_WIDTH), lambda i: (i, 0)),
                  pl.BlockSpec((tm, X_WIDTH), lambda i: (i, 0)),
                  pl.BlockSpec((BRANCH_WIDTH, d), lambda i: (0, 0)),
                  pl.BlockSpec((X_WIDTH, d), lambda i: (0, 0)),
                  pl.BlockSpec((1, d), lambda i: (0, 0)),
                  pl.BlockSpec((tm, d), lambda i: (i, 0))],
        out_specs=pl.BlockSpec((tm, d), lambda i: (i, 0)),
        out_shape=jax.ShapeDtypeStruct((t, d), F32),
        compiler_params=_params(("parallel",)),
        name="out_proj",
    )(y_seq, y_mem, ws, wm, g_post.reshape(1, d), h)


def _gla_body(q_ref, k_ref, v_ref, r_ref, glr_ref, wg_ref, bg_ref, gn_ref, tri_ref, y_ref, *scratch, chunks, heads):
    C = GLA_CHUNK
    per_head = len(scratch) // heads

    @pl.when(pl.program_id(1) == 0)
    def _():
        for hh in range(heads):
            st_ref = scratch[hh * per_head]
            st_ref[...] = jnp.zeros_like(st_ref)

    def head_chunk(hh, c):
        rows = pl.ds(pl.multiple_of(c * C, C), C)
        kc = slice(hh * GLA_DKP, (hh + 1) * GLA_DKP)
        vc = slice(hh * GLA_DV, (hh + 1) * GLA_DV)
        st_ref, b_h, q_h, k_h, v_h, o_h = scratch[hh * per_head:(hh + 1) * per_head]
        gp = jnp.dot(glr_ref[rows, :], wg_ref[hh], preferred_element_type=F32) + bg_ref[hh]
        logg = (jnp.minimum(gp, 0.0) - jnp.log1p(jnp.exp(-jnp.abs(gp)))) * (1.0 / GLA_TEMP)
        b = _split_dot_left(tri_ref[...], logg)
        q = q_ref[rows, kc].astype(F32) * (GLA_DK ** -0.5)
        k = k_ref[rows, kc].astype(F32)
        v = v_ref[rows, vc]
        b_h[...] = b
        q_h[...] = q
        k_h[...] = k
        v_h[...] = v.astype(F32)
        st = st_ref[...]
        o_h[...] = _dot_nt((q * jnp.exp(b)).astype(BF16), st.astype(BF16))
        yield

        SB = GLA_SUB
        for j in range(C // SB - 1):
            lo, hi = j * SB, (j + 1) * SB
            beta = b_h[hi:hi + 1, :]
            qt = (q_h[hi:, :] * jnp.exp(b_h[hi:, :] - beta)).astype(BF16)
            kt = (k_h[lo:hi, :] * jnp.exp(beta - b_h[lo:hi, :])).astype(BF16)
            a = _dot_nt(qt, kt)
            vj = v_ref[pl.ds(pl.multiple_of(c * C + lo, SB), SB), vc]
            o_h[hi:, :] += jnp.dot(a.astype(BF16), vj, preferred_element_type=F32)
            yield

        sub8 = lax.broadcasted_iota(jnp.int32, (8, 1), 0)
        for blk in range(C // SB):
            base = blk * SB
            acc = [jnp.zeros((8, GLA_DV), F32) for _ in range(SB // 8)]
            for s in range(SB):
                row = base + s
                bs = b_h[row:row + 1, :]
                ks = k_h[row:row + 1, :]
                vs = v_h[row:row + 1, :]
                for tix in range(s // 8, SB // 8):
                    r0 = base + tix * 8
                    e = jnp.exp(b_h[r0:r0 + 8, :] - bs)
                    if tix == s // 8:
                        e = jnp.where(sub8 >= s % 8, e, 0.0)
                    a = jnp.sum(q_h[r0:r0 + 8, :] * ks * e, axis=-1, keepdims=True)
                    acc[tix] = acc[tix] + a * vs
            for tix in range(SB // 8):
                r0 = base + tix * 8
                o_h[r0:r0 + 8, :] += acc[tix]
            yield

        bl = b[C - 1:C, :]
        kd = (k * jnp.exp(bl - b)).astype(BF16)
        st_ref[...] = jnp.exp(bl) * st + _dot_tn(v, kd)
        yield

        o = o_h[...]
        on = (o * lax.rsqrt(jnp.mean(o * o, axis=-1, keepdims=True) + EPS)) * gn_ref[...]
        y_ref[rows, vc] = (on * _silu(r_ref[rows, vc].astype(F32))).astype(y_ref.dtype)
        yield

    def chunk(c, carry):
        for _ in zip(*[head_chunk(hh, c) for hh in range(heads)]):
            pass
        return carry

    lax.fori_loop(0, chunks, chunk, 0)


def _gla_scan(proj, wg, bg, gnorm, tb, heads):
    t = proj.shape[0]
    C = GLA_CHUNK
    tri = jnp.asarray(np.tril(np.ones((C, C), np.float32)), BF16)
    kw, vw = heads * GLA_DKP, heads * GLA_DV
    qb, kb, vb, rb, gb = GLA_Q // kw, GLA_K // kw, GLA_V // vw, GLA_R // vw, GLA_GLR // GLA_RANKP
    return pl.pallas_call(
        functools.partial(_gla_body, chunks=tb // C, heads=heads),
        grid=(GLA_HEADS // heads, t // tb),
        in_specs=[pl.BlockSpec((tb, kw), lambda h, n: (n, qb + h)),
                  pl.BlockSpec((tb, kw), lambda h, n: (n, kb + h)),
                  pl.BlockSpec((tb, vw), lambda h, n: (n, vb + h)),
                  pl.BlockSpec((tb, vw), lambda h, n: (n, rb + h)),
                  pl.BlockSpec((tb, GLA_RANKP), lambda h, n: (n, gb)),
                  pl.BlockSpec((heads, GLA_RANKP, GLA_DKP), lambda h, n: (h, 0, 0)),
                  pl.BlockSpec((heads, 1, GLA_DKP), lambda h, n: (h, 0, 0)),
                  pl.BlockSpec((1, GLA_DV), lambda h, n: (0, 0)),
                  pl.BlockSpec((C, C), lambda h, n: (0, 0))],
        out_specs=pl.BlockSpec((tb, vw), lambda h, n: (n, h)),
        out_shape=jax.ShapeDtypeStruct((t, BRANCH_WIDTH), BF16),
        scratch_shapes=[pltpu.VMEM((GLA_DV, GLA_DKP), F32),
                        pltpu.VMEM((C, GLA_DKP), F32),
                        pltpu.VMEM((C, GLA_DKP), F32),
                        pltpu.VMEM((C, GLA_DKP), F32),
                        pltpu.VMEM((C, GLA_DV), F32),
                        pltpu.VMEM((C, GLA_DV), F32)] * heads,
        compiler_params=_params(("parallel", "arbitrary")),
        name="gla_scan",
    )(proj, proj, proj, proj, proj, wg, bg, gnorm.reshape(1, GLA_DV), tri)


def _pad_heads(w, heads, width, padded):
    d = w.shape[0]
    return jnp.pad(w.reshape(d, heads, width), ((0, 0), (0, 0), (0, padded - width))).reshape(d, heads * padded)


def _gla_layer_seq(xn_proj_fn, w_in, w_gate_lr, b_gate, g_norm):
    sizes = np.cumsum([GLA_HEADS * GLA_DK, GLA_HEADS * GLA_DK, BRANCH_WIDTH, GLA_RANK, BRANCH_WIDTH, X_WIDTH])
    wq, wk, wv, wglr, wr, wxq, wxg = jnp.split(w_in, sizes.tolist(), axis=1)
    w_cat = jnp.concatenate([
        wv, wr, _pad_heads(wq, GLA_HEADS, GLA_DK, GLA_DKP), _pad_heads(wk, GLA_HEADS, GLA_DK, GLA_DKP), wxq, wxg,
        jnp.pad(wglr, ((0, 0), (0, GLA_NP - GLA_GLR - GLA_RANK)))], axis=1).astype(BF16)
    proj = xn_proj_fn(w_cat, GLA_TN)
    wg = jnp.pad(w_gate_lr.reshape(GLA_RANK, GLA_HEADS, GLA_DK).transpose(1, 0, 2),
                 ((0, 0), (0, GLA_RANKP - GLA_RANK), (0, GLA_DKP - GLA_DK))).astype(BF16)
    bg = jnp.pad(b_gate.reshape(GLA_HEADS, 1, GLA_DK), ((0, 0), (0, 0), (0, GLA_DKP - GLA_DK)))
    y_seq = _gla_scan(proj, wg, bg, g_norm, GLA_TB, GLA_HEADS_PER_STEP)
    return proj, y_seq, GLA_XQ // X_WIDTH, GLA_XG // X_WIDTH


def _cmp_mlp_body(x_ref, pe_ref, w1_ref, w2_ref, o_ref):
    x = (x_ref[...].astype(F32) + pe_ref[...]).astype(BF16)
    hid = _silu(jnp.dot(x, w1_ref[...], preferred_element_type=F32))
    o_ref[...] = jnp.dot(hid.astype(BF16), w2_ref[...], preferred_element_type=F32).astype(o_ref.dtype)


def _cmp_mlp(blocks, pe, w1, w2, tr):
    _, rows, width = blocks.shape
    return pl.pallas_call(
        _cmp_mlp_body,
        grid=(2, rows // tr),
        in_specs=[pl.BlockSpec((None, tr, width), lambda a, i: (a, i, 0)),
                  pl.BlockSpec((None, 1, width), lambda a, i: (a, 0, 0)),
                  pl.BlockSpec((None, width, CMP_HIDDEN), lambda a, i: (a, 0, 0)),
                  pl.BlockSpec((None, CMP_HIDDEN, HEAD_DIM), lambda a, i: (a, 0, 0))],
        out_specs=pl.BlockSpec((None, tr, HEAD_DIM), lambda a, i: (a, i, 0)),
        out_shape=jax.ShapeDtypeStruct((2, rows, HEAD_DIM), BF16),
        compiler_params=_params(("parallel", "parallel")),
        name="cmp_mlp",
    )(blocks, pe, w1, w2)


def _cmp_attn_body(q_ref, kct_ref, vca_ref, ovt_ref, o_ref, imp_ref, qs_sc, imp_sc, *, tq, widths):
    i = pl.program_id(0)
    n_tiles = pl.num_programs(0)
    for hh in range(NSA_HEADS):
        c = slice(hh * HEAD_DIM, (hh + 1) * HEAD_DIM)
        qs_sc[hh] = (q_ref[:, c].astype(F32) * (ATT_SCALE * LOG2E)).astype(BF16)

    def run(w):
        t = i * tq + lax.broadcasted_iota(jnp.int32, (tq, 1), 0)
        n = lax.broadcasted_iota(jnp.int32, (1, w), 1)
        bias = jnp.where(n * CMP_STRIDE + (CMP_LEN - 1) <= t, 0.0, NEG)
        has_key = t >= CMP_LEN - 1
        R = CMP_ROWS
        for g in range(NSA_GROUPS):
            for h in range(NSA_HPG):
                hh = g * NSA_HPG + h
                c = slice(hh * HEAD_DIM, (hh + 1) * HEAD_DIM)
                for r in range(tq // R):
                    rows = slice(r * R, (r + 1) * R)
                    s = jnp.dot(qs_sc[hh, rows, :], kct_ref[g, :, :w], preferred_element_type=F32) + bias[rows, :]
                    e = jnp.exp2(s - jnp.max(s, axis=-1, keepdims=True))
                    pv = jnp.dot(e.astype(BF16), vca_ref[g, :w, :], preferred_element_type=F32)
                    inv = jnp.where(has_key[rows, :], 1.0 / pv[:, HEAD_DIM:], 0.0)
                    o_ref[rows, c] = (pv[:, :HEAD_DIM] * inv).astype(o_ref.dtype)
                    p = e * _rep(inv, w)
                    if h == 0:
                        imp_sc[rows, :w] = p
                    else:
                        imp_sc[rows, :w] += p
            imp_ref[g] = _split_dot_nt(ovt_ref[:, :w], imp_sc[:, :w])

    per_class = n_tiles // len(widths)
    for cls, w in enumerate(widths):
        pl.when(i // per_class == cls)(functools.partial(run, w))


def _cmp_attn(proj, kct, vca, overlap_t, tq):
    t = proj.shape[0]
    nb, nc = overlap_t.shape
    classes = 4
    widths = tuple(nc * (c + 1) // classes for c in range(classes))
    assert (t // tq) % classes == 0 and all(w % min(LANES, nc // classes) == 0 for w in widths)
    return pl.pallas_call(
        functools.partial(_cmp_attn_body, tq=tq, widths=widths),
        grid=(t // tq,),
        in_specs=[pl.BlockSpec((tq, BRANCH_WIDTH), lambda i: (i, NSA_Q // BRANCH_WIDTH)),
                  pl.BlockSpec((NSA_GROUPS, HEAD_DIM, nc), lambda i: (0, 0, 0)),
                  pl.BlockSpec((NSA_GROUPS, nc, 2 * HEAD_DIM), lambda i: (0, 0, 0)),
                  pl.BlockSpec((nb, nc), lambda i: (0, 0))],
        out_specs=[pl.BlockSpec((tq, BRANCH_WIDTH), lambda i: (i, 0)),
                   pl.BlockSpec((NSA_GROUPS, nb, tq), lambda i: (0, 0, i))],
        out_shape=[jax.ShapeDtypeStruct((t, BRANCH_WIDTH), BF16),
                   jax.ShapeDtypeStruct((NSA_GROUPS, nb, t), F32)],
        scratch_shapes=[pltpu.VMEM((NSA_HEADS, tq, HEAD_DIM), BF16),
                        pltpu.VMEM((tq, nc), F32)],
        compiler_params=_params(("parallel",)),
        name="cmp_attn",
    )(proj, kct, vca, overlap_t)


def _topk_body(imp_ref, sel_ref, x_sc, *, tr, nb):
    t = pl.program_id(1) * tr + lax.broadcasted_iota(jnp.int32, (1, tr), 1)
    jb = lax.broadcasted_iota(jnp.int32, (nb, 1), 0)
    cur = jnp.right_shift(t, SLC_SHIFT)
    x = imp_ref[...]
    x = jnp.where((jb == 0) | (jb == cur) | (jb == cur - 1), FORCED, x)
    x_sc[...] = jnp.where(jb > cur, -FORCED, x)
    R = min(TOPK_LANES, tr)
    jbf = lax.broadcasted_iota(jnp.int32, (nb, R), 0).astype(F32)

    def pick(_, carry):
        for r in range(tr // R):
            cols = slice(r * R, (r + 1) * R)
            x = x_sc[:, cols]
            m = jnp.max(x, axis=0, keepdims=True)
            idx = jnp.min(jnp.where(x == m, jbf, float(nb)), axis=0, keepdims=True)
            x_sc[:, cols] = jnp.where(jbf == idx, PICKED, x)
        return carry

    lax.fori_loop(0, SLC_TOPN, pick, 0)
    sel_ref[...] = jnp.where(x_sc[...] < 0.5 * PICKED, 0.0, NEG).astype(sel_ref.dtype)


def _topk_select(imp_t, tr):
    groups, nb, t = imp_t.shape
    return pl.pallas_call(
        functools.partial(_topk_body, tr=tr, nb=nb),
        grid=(groups, t // tr),
        in_specs=[pl.BlockSpec((None, nb, tr), lambda g, i: (g, 0, i))],
        out_specs=pl.BlockSpec((None, nb, tr), lambda g, i: (g, 0, i)),
        out_shape=jax.ShapeDtypeStruct(imp_t.shape, BF16),
        scratch_shapes=[pltpu.VMEM((nb, tr), F32)],
        compiler_params=_params(("parallel", "parallel")),
        name="topk_select",
    )(imp_t)


def _win_attn_body(q_ref, kp_ref, kc_ref, vp_ref, vc_ref, o_ref, qs_sc, vpa_sc, vca_sc, *, tq):
    i = pl.program_id(1)
    R = WIN_ROWS
    for h in range(NSA_HPG):
        qs_sc[h] = (q_ref[:, h * HEAD_DIM:(h + 1) * HEAD_DIM].astype(F32) * (ATT_SCALE * LOG2E)).astype(BF16)
    for src, dst in ((vp_ref, vpa_sc), (vc_ref, vca_sc)):
        dst[:, :HEAD_DIM] = src[...]
        dst[:, HEAD_DIM:] = jnp.ones((tq, HEAD_DIM), BF16)
    col = lax.broadcasted_iota(jnp.int32, (1, R), 1)
    row = lax.broadcasted_iota(jnp.int32, (R, 1), 0)
    upto_query = jnp.where(col <= row, 0.0, NEG)
    inside_window = jnp.where(col > row, 0.0, NEG)

    def run(with_prev):
        for h in range(NSA_HPG):
            for r in range(tq // R):
                rows = slice(r * R, (r + 1) * R)
                n_cur = (r + 1) * R
                sc = jnp.dot(qs_sc[h, rows, :], kc_ref[:, :n_cur], preferred_element_type=F32)
                parts = ([sc[:, :n_cur - R]] if r > 0 else []) + [sc[:, n_cur - R:] + upto_query]
                if with_prev:
                    n_prev = tq - r * R
                    sp = jnp.dot(qs_sc[h, rows, :], kp_ref[:, r * R:], preferred_element_type=F32)
                    parts = [sp[:, :R] + inside_window] + ([sp[:, R:]] if n_prev > R else []) + parts
                s = parts[0] if len(parts) == 1 else jnp.concatenate(parts, axis=1)
                e = jnp.exp2(s - jnp.max(s, axis=-1, keepdims=True)).astype(BF16)
                if with_prev:
                    pv = (jnp.dot(e[:, :n_prev], vpa_sc[r * R:, :], preferred_element_type=F32)
                          + jnp.dot(e[:, n_prev:], vca_sc[:n_cur, :], preferred_element_type=F32))
                else:
                    pv = jnp.dot(e, vca_sc[:n_cur, :], preferred_element_type=F32)
                o_ref[rows, h * HEAD_DIM:(h + 1) * HEAD_DIM] = (pv[:, :HEAD_DIM] / pv[:, HEAD_DIM:]).astype(o_ref.dtype)

    pl.when(i == 0)(functools.partial(run, False))
    pl.when(i > 0)(functools.partial(run, True))


def _win_attn(proj, kt, tq):
    t = proj.shape[0]
    assert tq == WINDOW
    prev = lambda i: jnp.maximum(i - 1, 0)
    vb = NSA_VW // HEAD_DIM
    return pl.pallas_call(
        functools.partial(_win_attn_body, tq=tq),
        grid=(NSA_GROUPS, t // tq),
        in_specs=[pl.BlockSpec((tq, NSA_GW), lambda g, i: (i, g)),
                  pl.BlockSpec((HEAD_DIM, tq), lambda g, i: (NSA_KT_WIN + g, prev(i))),
                  pl.BlockSpec((HEAD_DIM, tq), lambda g, i: (NSA_KT_WIN + g, i)),
                  pl.BlockSpec((tq, HEAD_DIM), lambda g, i: (prev(i), vb + g)),
                  pl.BlockSpec((tq, HEAD_DIM), lambda g, i: (i, vb + g))],
        out_specs=pl.BlockSpec((tq, NSA_GW), lambda g, i: (i, g)),
        out_shape=jax.ShapeDtypeStruct((t, BRANCH_WIDTH), BF16),
        scratch_shapes=[pltpu.VMEM((NSA_HPG, tq, HEAD_DIM), BF16),
                        pltpu.VMEM((tq, 2 * HEAD_DIM), BF16),
                        pltpu.VMEM((tq, 2 * HEAD_DIM), BF16)],
        compiler_params=_params(("parallel", "parallel")),
        name="win_attn",
    )(proj, kt, kt, proj, proj)


def _slc_attn_body(it_ref, jt_ref, q_ref, k_ref, v_ref, sel_ref, ocmp_ref, owin_ref, gl_ref, r_ref, y_ref,
                   qa_sc, m_sc, acc_sc, ka_sc, va_sc, *, tq, tk, lw):
    step = pl.program_id(1)
    i = it_ref[step]
    j = jt_ref[step]

    @pl.when(j == 0)
    def _():
        for h in range(NSA_HPG):
            cs = slice(h * HEAD_DIM, (h + 1) * HEAD_DIM)
            qa_sc[h, :, :HEAD_DIM] = (q_ref[:, cs].astype(F32) * (ATT_SCALE * LOG2E)).astype(BF16)
        m_sc[...] = jnp.full_like(m_sc, NEG)
        acc_sc[...] = jnp.zeros_like(acc_sc)

    sel = sel_ref[...]
    for h in range(NSA_HPG):
        qa_sc[h, :, HEAD_DIM:] = sel
    key_blk = (j * (tk // SLC_LEN) + jnp.right_shift(lax.broadcasted_iota(jnp.int32, (1, tk), 1), SLC_SHIFT)) & (lw - 1)
    ka_sc[:HEAD_DIM, :] = k_ref[...]
    ka_sc[HEAD_DIM:, :] = jnp.where(lax.broadcasted_iota(jnp.int32, (lw, 1), 0) == key_blk, 1.0, 0.0).astype(BF16)
    va_sc[:, :HEAD_DIM] = v_ref[...]
    va_sc[:, HEAD_DIM:] = jnp.ones((tk, HEAD_DIM), BF16)

    def accumulate(diag):
        R = SLC_ROWS
        if diag:
            tri = jnp.where(lax.broadcasted_iota(jnp.int32, (1, R), 1) <= lax.broadcasted_iota(jnp.int32, (R, 1), 0),
                            0.0, NEG)
        for h in range(NSA_HPG):
            for r in range(tq // R):
                rows = slice(r * R, (r + 1) * R)
                ncols = (r + 1) * R if diag else tk
                s = jnp.dot(qa_sc[h, rows, :], ka_sc[:, :ncols], preferred_element_type=F32)
                if diag:
                    tail = s[:, ncols - R:] + tri
                    s = tail if r == 0 else jnp.concatenate([s[:, :ncols - R], tail], axis=1)
                m_old = m_sc[h, rows, :]
                m_new = jnp.maximum(m_old, jnp.max(s, axis=-1, keepdims=True))
                p = jnp.exp2(s - _rep(m_new, ncols))
                alpha = jnp.exp2(m_old - m_new)
                acc_sc[h, rows, :] = (_rep(alpha, 2 * HEAD_DIM) * acc_sc[h, rows, :]
                                      + jnp.dot(p.astype(BF16), va_sc[:ncols, :], preferred_element_type=F32))
                m_sc[h, rows, :] = m_new

    @pl.when(j < i)
    def _():
        accumulate(False)

    @pl.when(j == i)
    def _():
        accumulate(True)
        gates = jax.nn.sigmoid(gl_ref[...].astype(F32))
        for h in range(NSA_HPG):
            cs = slice(h * HEAD_DIM, (h + 1) * HEAD_DIM)
            o_slc = acc_sc[h, :, :HEAD_DIM] / acc_sc[h, :, HEAD_DIM:]
            g0 = gates[:, h * N_BRANCH + 0:h * N_BRANCH + 1]
            g1 = gates[:, h * N_BRANCH + 1:h * N_BRANCH + 2]
            g2 = gates[:, h * N_BRANCH + 2:h * N_BRANCH + 3]
            o = g0 * ocmp_ref[:, cs].astype(F32) + g1 * o_slc + g2 * owin_ref[:, cs].astype(F32)
            y_ref[:, cs] = (o * _silu(r_ref[:, cs].astype(F32))).astype(y_ref.dtype)


def _slc_attn(proj, kt, sel, o_cmp, o_win, tile):
    t = proj.shape[0]
    nb = sel.shape[1] // NSA_GROUPS
    lw = min(LANES, nb)
    assert lw & (lw - 1) == 0 and (tile // SLC_LEN) <= lw
    ni = t // tile
    it = np.concatenate([np.full(i + 1, i) for i in range(ni)]).astype(np.int32)
    jt = np.concatenate([np.arange(i + 1) for i in range(ni)]).astype(np.int32)
    blocks_per_tile = tile // SLC_LEN
    sel_idx = lambda g, s, it, jt: (it[s], g * (nb // lw) + (jt[s] * blocks_per_tile) // lw)
    row = lambda g, s, it, jt: (it[s], g)
    grid_spec = pltpu.PrefetchScalarGridSpec(
        num_scalar_prefetch=2,
        grid=(NSA_GROUPS, len(it)),
        in_specs=[pl.BlockSpec((tile, NSA_GW), row),
                  pl.BlockSpec((HEAD_DIM, tile), lambda g, s, it, jt: (NSA_KT_SLC + g, jt[s])),
                  pl.BlockSpec((tile, HEAD_DIM), lambda g, s, it, jt: (jt[s], NSA_VS // HEAD_DIM + g)),
                  pl.BlockSpec((tile, lw), sel_idx),
                  pl.BlockSpec((tile, NSA_GW), row),
                  pl.BlockSpec((tile, NSA_GW), row),
                  pl.BlockSpec((tile, HEAD_DIM), lambda g, s, it, jt: (it[s], NSA_GL // HEAD_DIM + g)),
                  pl.BlockSpec((tile, NSA_GW), lambda g, s, it, jt: (it[s], NSA_R // NSA_GW + g))],
        out_specs=pl.BlockSpec((tile, NSA_GW), row),
        scratch_shapes=[pltpu.VMEM((NSA_HPG, tile, HEAD_DIM + lw), BF16),
                        pltpu.VMEM((NSA_HPG, tile, HEAD_DIM), F32),
                        pltpu.VMEM((NSA_HPG, tile, 2 * HEAD_DIM), F32),
                        pltpu.VMEM((HEAD_DIM + lw, tile), BF16),
                        pltpu.VMEM((tile, 2 * HEAD_DIM), BF16)])
    return pl.pallas_call(
        functools.partial(_slc_attn_body, tq=tile, tk=tile, lw=lw),
        grid_spec=grid_spec,
        out_shape=jax.ShapeDtypeStruct((t, BRANCH_WIDTH), BF16),
        compiler_params=_params(("parallel", "arbitrary")),
        name="slc_attn",
    )(jnp.asarray(it), jnp.asarray(jt), proj, kt, proj, sel, o_cmp, o_win, proj, proj)


def _nsa_layer_seq(xn_proj_fn, w_in, pe_k, pe_v, wk1, wk2, wv1, wv2):
    t_sizes = [BRANCH_WIDTH] + [NSA_GROUPS * HEAD_DIM] * 6 + [NSA_HEADS * N_BRANCH, BRANCH_WIDTH, X_WIDTH]
    wq, wkc, wvc, wks, wvs, wkw, wvw, wgl, wr, wxq, wxg = jnp.split(w_in, np.cumsum(t_sizes).tolist(), axis=1)
    wgl = _pad_heads(wgl, NSA_GROUPS, NSA_HPG * N_BRANCH, HEAD_DIM)
    w_cat = jnp.concatenate([wq, wr, wkc, wvc, wvs, wvw, wxq, wxg, wgl], axis=1).astype(BF16)
    wt = jnp.concatenate([wks, wkw], axis=1).T.astype(BF16)
    proj, kt = xn_proj_fn(w_cat, NSA_TN, wt)
    t = proj.shape[0]

    half = CMP_STRIDE
    nc = t // half
    a = proj[:, NSA_KC:NSA_VS].reshape(nc, half, 2, NSA_GROUPS, HEAD_DIM)
    a = a.transpose(2, 3, 0, 1, 4).reshape(2, NSA_GROUPS, nc, half * HEAD_DIM)
    blocks = jnp.concatenate([a, jnp.roll(a, -1, axis=2)], axis=-1).reshape(2, NSA_GROUPS * nc, CMP_LEN * HEAD_DIM)
    pe = jnp.stack([pe_k, pe_v]).reshape(2, 1, CMP_LEN * HEAD_DIM)
    kv_cmp = _cmp_mlp(blocks, pe, jnp.stack([wk1, wv1]).astype(BF16), jnp.stack([wk2, wv2]).astype(BF16),
                      min(CMP_MLP_TR, NSA_GROUPS * nc))
    kv_cmp = kv_cmp.reshape(2, NSA_GROUPS, nc, HEAD_DIM)

    nb = t // SLC_LEN
    cstart = np.arange(nc) * CMP_STRIDE
    sstart = np.arange(nb) * SLC_LEN
    overlap = ((cstart[:, None] < sstart[None, :] + SLC_LEN) & (cstart[:, None] + CMP_LEN > sstart[None, :]))
    overlap[nc - 1] = False
    kct = kv_cmp[0].transpose(0, 2, 1)
    vca = jnp.concatenate([kv_cmp[1], jnp.ones_like(kv_cmp[1])], axis=-1)
    o_cmp, imp_t = _cmp_attn(proj, kct, vca, jnp.asarray(overlap.T.astype(np.float32), BF16), CMP_TQ)
    sel = _topk_select(imp_t, min(TOPK_TR, t)).transpose(2, 0, 1).reshape(t, NSA_GROUPS * nb)
    o_win = _win_attn(proj, kt, WINDOW)
    y_seq = _slc_attn(proj, kt, sel, o_cmp, o_win, SLC_TILE)
    return proj, y_seq, NSA_XQ // X_WIDTH, NSA_XG // X_WIDTH


def kernel(x, mem, ln_pre, ln_post, ln_mem, w_mem_kv, gla_w_in, gla_w_gate_lr, gla_b_gate, gla_norm, gla_w_out,
           nsa_w_in, nsa_pe_k, nsa_pe_v, nsa_wk1, nsa_wk2, nsa_wv1, nsa_wv2, nsa_w_out):
    batch, t, d = x.shape
    assert batch == 1 and d == D_MODEL and t % PROJ_TM == 0
    h = x.reshape(t, d)
    mem2 = mem.reshape(mem.shape[1], d)
    depth = ln_pre.shape[0]
    for i in range(depth):
        a = i // 2
        xn_proj_fn = lambda w, tn, wt=None, i=i, h=h: _norm_proj(h, ln_pre[i], w, PROJ_TM, tn, wt)
        kv = _norm_proj(mem2, ln_mem[i], w_mem_kv[i].astype(BF16), mem2.shape[0], 2 * X_WIDTH)
        if i % 2 == 0:
            proj, y_seq, xq_blk, xg_blk = _gla_layer_seq(xn_proj_fn, gla_w_in[a], gla_w_gate_lr[a], gla_b_gate[a],
                                                         gla_norm[a])
            w_out = gla_w_out[a]
        else:
            proj, y_seq, xq_blk, xg_blk = _nsa_layer_seq(xn_proj_fn, nsa_w_in[a], nsa_pe_k[a], nsa_pe_v[a],
                                                         nsa_wk1[a], nsa_wk2[a], nsa_wv1[a], nsa_wv2[a])
            w_out = nsa_w_out[a]
        y_mem = _mem_attn(proj, kv, xq_blk, xg_blk, MEM_TQ)
        h = _out_proj(y_seq, y_mem, w_out, ln_post[i], h, OUT_TM)
    return h.reshape(batch, t, d)
```

```python
import functools
import itertools

import numpy as np
import jax
import jax.numpy as jnp
from jax import lax
from jax.experimental import pallas as pl
from jax.experimental.pallas import tpu as pltpu

F32 = jnp.float32
BF16 = jnp.bfloat16

D_MODEL = 2048
HEAD_DIM = 128
EPS = 1e-6
X_HEADS = 4
X_WIDTH = X_HEADS * HEAD_DIM
BRANCH_WIDTH = D_MODEL - X_WIDTH
GLA_HEADS = 4
GLA_DV = BRANCH_WIDTH // GLA_HEADS
GLA_DK = GLA_DV // 2
GLA_DKP = 256
GLA_RANK = 16
GLA_RANKP = 128
GLA_TEMP = 16.0
GLA_CHUNK = 64
GLA_SUB = 16
NSA_HEADS = BRANCH_WIDTH // HEAD_DIM
NSA_GROUPS = 2
NSA_HPG = NSA_HEADS // NSA_GROUPS
NSA_GW = NSA_HPG * HEAD_DIM
N_BRANCH = 3
CMP_LEN = 32
CMP_STRIDE = 16
CMP_HIDDEN = 256
SLC_LEN = 64
SLC_SHIFT = SLC_LEN.bit_length() - 1
SLC_TOPN = 16
WINDOW = 512
FORCED = 1e4
NEG = -1e30
PICKED = -3e38
ATT_SCALE = HEAD_DIM ** -0.5
LOG2E = 1.4426950408889634

LANES = 128
VMEM_LIMIT = 56 * 1024 * 1024

PROJ_TM = 1024
GLA_TN, NSA_TN = 1280, 1792
GLA_TB = 512
GLA_HEADS_PER_STEP = 2
MEM_TQ = 512
OUT_TM = 256
CMP_MLP_TR = 512
CMP_TQ = 256
TOPK_TR = 1024
SLC_TILE = 1024
SLC_ROWS = 256
CMP_ROWS = 128
WIN_ROWS = 256
TOPK_LANES = 128

GLA_V, GLA_R, GLA_Q, GLA_K, GLA_XQ, GLA_XG, GLA_GLR, GLA_NP = 0, 1536, 3072, 4096, 5120, 5632, 6144, 6400
(NSA_Q, NSA_R, NSA_KC, NSA_VC, NSA_VS, NSA_VW, NSA_XQ, NSA_XG, NSA_GL, NSA_NP) = (
    0, 1536, 3072, 3328, 3584, 3840, 4096, 4608, 5120, 5376)
NSA_KT_SLC, NSA_KT_WIN = 0, NSA_GROUPS


def _params(sem):
    return pltpu.CompilerParams(dimension_semantics=sem, vmem_limit_bytes=VMEM_LIMIT)


def _silu(x):
    return x * jax.nn.sigmoid(x)


def _dot_nt(a, b):
    return lax.dot_general(a, b, (((1,), (1,)), ((), ())), preferred_element_type=F32)


def _dot_tn(a, b):
    return lax.dot_general(a, b, (((0,), (0,)), ((), ())), preferred_element_type=F32)


def _rep(x, width):
    return x[:, :width] if width < LANES else jnp.concatenate([x] * (width // LANES), axis=1)


def _split_dot_nt(w, x):
    hi = x.astype(BF16)
    lo = (x - hi.astype(F32)).astype(BF16)
    return _dot_nt(w, hi) + _dot_nt(w, lo)


def _split_dot_left(w, x):
    hi = x.astype(BF16)
    lo = (x - hi.astype(F32)).astype(BF16)
    return (jnp.dot(w, hi, preferred_element_type=F32) + jnp.dot(w, lo, preferred_element_type=F32))


def _norm_proj_body(x_ref, g_ref, w_ref, *rest, transposed):
    if transposed:
        wt_ref, o_ref, ot_ref, xn_ref = rest
    else:
        o_ref, xn_ref = rest

    @pl.when(pl.program_id(1) == 0)
    def _():
        x = x_ref[...]
        r = lax.rsqrt(jnp.mean(x * x, axis=-1, keepdims=True) + EPS)
        xn_ref[...] = ((x * r) * g_ref[...]).astype(BF16)
        if transposed:
            ot_ref[...] = _dot_nt(wt_ref[...], xn_ref[...]).astype(ot_ref.dtype)

    o_ref[...] = jnp.dot(xn_ref[...], w_ref[...], preferred_element_type=F32).astype(o_ref.dtype)


def _norm_proj(x, g, w, tm, tn, wt=None):
    rows, d = x.shape
    n = w.shape[1]
    in_specs = [pl.BlockSpec((tm, d), lambda i, j: (i, 0)),
                pl.BlockSpec((1, d), lambda i, j: (0, 0)),
                pl.BlockSpec((d, tn), lambda i, j: (0, j))]
    out_specs = pl.BlockSpec((tm, tn), lambda i, j: (i, j))
    out_shape = jax.ShapeDtypeStruct((rows, n), BF16)
    args = (x, g.reshape(1, d), w)
    if wt is not None:
        nt = wt.shape[0]
        in_specs.append(pl.BlockSpec((nt, d), lambda i, j: (0, 0)))
        out_specs = [out_specs, pl.BlockSpec((nt, tm), lambda i, j: (0, i))]
        out_shape = [out_shape, jax.ShapeDtypeStruct((nt, rows), BF16)]
        args = args + (wt,)
    return pl.pallas_call(
        functools.partial(_norm_proj_body, transposed=wt is not None),
        grid=(rows // tm, n // tn),
        in_specs=in_specs,
        out_specs=out_specs,
        out_shape=out_shape,
        scratch_shapes=[pltpu.VMEM((tm, d), BF16)],
        compiler_params=_params(("parallel", "arbitrary")),
        name="norm_proj",
    )(*args)


def _mem_attn_body(xq_ref, xg_ref, kv_ref, o_ref):
    for h in range(X_HEADS):
        c = slice(h * HEAD_DIM, (h + 1) * HEAD_DIM)
        k = kv_ref[:, c]
        v = kv_ref[:, X_WIDTH + h * HEAD_DIM: X_WIDTH + (h + 1) * HEAD_DIM]
        s = _dot_nt(xq_ref[:, c], k) * ATT_SCALE
        e = jnp.exp(s - jnp.max(s, axis=-1, keepdims=True))
        p = e / jnp.sum(e, axis=-1, keepdims=True)
        o = jnp.dot(p.astype(BF16), v, preferred_element_type=F32)
        o_ref[:, c] = (o * _silu(xg_ref[:, c].astype(F32))).astype(o_ref.dtype)


def _mem_attn(proj, kv, xq_blk, xg_blk, tq):
    t = proj.shape[0]
    m = kv.shape[0]
    return pl.pallas_call(
        _mem_attn_body,
        grid=(t // tq,),
        in_specs=[pl.BlockSpec((tq, X_WIDTH), lambda i: (i, xq_blk)),
                  pl.BlockSpec((tq, X_WIDTH), lambda i: (i, xg_blk)),
                  pl.BlockSpec((m, 2 * X_WIDTH), lambda i: (0, 0))],
        out_specs=pl.BlockSpec((tq, X_WIDTH), lambda i: (i, 0)),
        out_shape=jax.ShapeDtypeStruct((t, X_WIDTH), BF16),
        compiler_params=_params(("parallel",)),
        name="mem_attn",
    )(proj, proj, kv)


def _out_proj_body(ys_ref, ym_ref, ws_ref, wm_ref, g_ref, h_ref, o_ref):
    y = (jnp.dot(ys_ref[...], ws_ref[...], preferred_element_type=F32)
         + jnp.dot(ym_ref[...], wm_ref[...], preferred_element_type=F32))
    r = lax.rsqrt(jnp.mean(y * y, axis=-1, keepdims=True) + EPS)
    o_ref[...] = h_ref[...] + (y * r) * g_ref[...]


def _out_proj(y_seq, y_mem, w_out, g_post, h, tm):
    t, d = h.shape
    ws = w_out[:BRANCH_WIDTH].astype(BF16)
    wm = w_out[BRANCH_WIDTH:].astype(BF16)
    return pl.pallas_call(
        _out_proj_body,
        grid=(t // tm,),
        in_specs=[pl.BlockSpec((tm, BRANCH_WIDTH), lambda i: (i, 0)),
                  pl.BlockSpec((tm, X_WIDTH), lambda i: (i, 0)),
                  pl.BlockSpec((BRANCH_WIDTH, d), lambda i: (0, 0)),
                  pl.BlockSpec((X_WIDTH, d), lambda i: (0, 0)),
                  pl.BlockSpec((1, d), lambda i: (0, 0)),
                  pl.BlockSpec((tm, d), lambda i: (i, 0))],
        out_specs=pl.BlockSpec((tm, d), lambda i: (i, 0)),
        out_shape=jax.ShapeDtypeStruct((t, d), F32),
        compiler_params=_params(("parallel",)),
        name="out_proj",
    )(y_seq, y_mem, ws, wm, g_post.reshape(1, d), h)


def _gla_body(q_ref, k_ref, v_ref, r_ref, glr_ref, wg_ref, bg_ref, gn_ref, tri_ref, y_ref, *scratch, chunks, heads):
    C = GLA_CHUNK
    SB = GLA_SUB
    per_head = len(scratch) // heads
    state = [scratch[hh * per_head] for hh in range(heads)]
    bufs = [[scratch[hh * per_head + 1 + 4 * par: hh * per_head + 5 + 4 * par] for par in range(2)]
            for hh in range(heads)]

    @pl.when(pl.program_id(1) == 0)
    def _():
        for st_ref in state:
            st_ref[...] = jnp.zeros_like(st_ref)

    def free_part(hh, c, par):
        rows = pl.ds(pl.multiple_of(c * C, C), C)
        kc = slice(hh * GLA_DKP, (hh + 1) * GLA_DKP)
        vc = slice(hh * GLA_DV, (hh + 1) * GLA_DV)
        b_h, q_h, k_h, o_h = bufs[hh][par]
        gp = jnp.dot(glr_ref[rows, :], wg_ref[hh], preferred_element_type=F32) + bg_ref[hh]
        logg = (jnp.minimum(gp, 0.0) - jnp.log1p(jnp.exp(-jnp.abs(gp)))) * (1.0 / GLA_TEMP)
        b_h[...] = _split_dot_left(tri_ref[...], logg)
        q_h[...] = q_ref[rows, kc].astype(F32) * (GLA_DK ** -0.5)
        k_h[...] = k_ref[rows, kc].astype(F32)
        o_h[0:SB, :] = jnp.zeros((SB, GLA_DV), F32)
        yield

        for j in range(C // SB - 1):
            lo, hi = j * SB, (j + 1) * SB
            beta = b_h[hi:hi + 1, :]
            qt = (q_h[hi:, :] * jnp.exp(b_h[hi:, :] - beta)).astype(BF16)
            kt = (k_h[lo:hi, :] * jnp.exp(beta - b_h[lo:hi, :])).astype(BF16)
            a = _dot_nt(qt, kt)
            vj = v_ref[pl.ds(pl.multiple_of(c * C + lo, SB), SB), vc]
            contrib = jnp.dot(a.astype(BF16), vj, preferred_element_type=F32)
            if j == 0:
                o_h[hi:, :] = contrib
            else:
                o_h[hi:, :] += contrib
            yield

        sub8 = lax.broadcasted_iota(jnp.int32, (8, 1), 0)
        for blk in range(C // SB):
            base = blk * SB
            vblk = v_ref[pl.ds(pl.multiple_of(c * C + base, SB), SB), vc].astype(F32)
            acc = [jnp.zeros((8, GLA_DV), F32) for _ in range(SB // 8)]
            for s in range(SB):
                row = base + s
                bs = b_h[row:row + 1, :]
                ks = k_h[row:row + 1, :]
                vs = vblk[s:s + 1, :]
                for tix in range(s // 8, SB // 8):
                    r0 = base + tix * 8
                    e = jnp.exp(b_h[r0:r0 + 8, :] - bs)
                    if tix == s // 8:
                        e = jnp.where(sub8 >= s % 8, e, 0.0)
                    a = jnp.sum(q_h[r0:r0 + 8, :] * ks * e, axis=-1, keepdims=True)
                    acc[tix] = acc[tix] + a * vs
            for tix in range(SB // 8):
                r0 = base + tix * 8
                o_h[r0:r0 + 8, :] += acc[tix]
            yield

    def state_part(hh, c, par):
        rows = pl.ds(pl.multiple_of(c * C, C), C)
        vc = slice(hh * GLA_DV, (hh + 1) * GLA_DV)
        b_h, q_h, k_h, o_h = bufs[hh][par]
        st_ref = state[hh]
        b = b_h[...]
        st = st_ref[...]
        o = o_h[...] + _dot_nt((q_h[...] * jnp.exp(b)).astype(BF16), st.astype(BF16))
        yield
        bl = b[C - 1:C, :]
        kd = (k_h[...] * jnp.exp(bl - b)).astype(BF16)
        st_ref[...] = jnp.exp(bl) * st + _dot_tn(v_ref[rows, vc], kd)
        yield
        on = (o * lax.rsqrt(jnp.mean(o * o, axis=-1, keepdims=True) + EPS)) * gn_ref[...]
        y_ref[rows, vc] = (on * _silu(r_ref[rows, vc].astype(F32))).astype(y_ref.dtype)

    def run(*gens):
        for _ in itertools.zip_longest(*gens):
            pass

    def pair(c, with_next):
        run(*[state_part(hh, c, 0) for hh in range(heads)], *[free_part(hh, c + 1, 1) for hh in range(heads)])
        run(*[state_part(hh, c + 1, 1) for hh in range(heads)],
            *([free_part(hh, c + 2, 0) for hh in range(heads)] if with_next else []))

    run(*[free_part(hh, 0, 0) for hh in range(heads)])

    def body(c2, carry):
        pair(2 * c2, True)
        return carry

    lax.fori_loop(0, chunks // 2 - 1, body, 0)
    pair(chunks - 2, False)


def _gla_scan(proj, wg, bg, gnorm, tb, heads):
    t = proj.shape[0]
    C = GLA_CHUNK
    tri = jnp.asarray(np.tril(np.ones((C, C), np.float32)), BF16)
    kw, vw = heads * GLA_DKP, heads * GLA_DV
    qb, kb, vb, rb, gb = GLA_Q // kw, GLA_K // kw, GLA_V // vw, GLA_R // vw, GLA_GLR // GLA_RANKP
    return pl.pallas_call(
        functools.partial(_gla_body, chunks=tb // C, heads=heads),
        grid=(GLA_HEADS // heads, t // tb),
        in_specs=[pl.BlockSpec((tb, kw), lambda h, n: (n, qb + h)),
                  pl.BlockSpec((tb, kw), lambda h, n: (n, kb + h)),
                  pl.BlockSpec((tb, vw), lambda h, n: (n, vb + h)),
                  pl.BlockSpec((tb, vw), lambda h, n: (n, rb + h)),
                  pl.BlockSpec((tb, GLA_RANKP), lambda h, n: (n, gb)),
                  pl.BlockSpec((heads, GLA_RANKP, GLA_DKP), lambda h, n: (h, 0, 0)),
                  pl.BlockSpec((heads, 1, GLA_DKP), lambda h, n: (h, 0, 0)),
                  pl.BlockSpec((1, GLA_DV), lambda h, n: (0, 0)),
                  pl.BlockSpec((C, C), lambda h, n: (0, 0))],
        out_specs=pl.BlockSpec((tb, vw), lambda h, n: (n, h)),
        out_shape=jax.ShapeDtypeStruct((t, BRANCH_WIDTH), BF16),
        scratch_shapes=([pltpu.VMEM((GLA_DV, GLA_DKP), F32)]
                        + [pltpu.VMEM((C, GLA_DKP), F32), pltpu.VMEM((C, GLA_DKP), F32),
                           pltpu.VMEM((C, GLA_DKP), F32), pltpu.VMEM((C, GLA_DV), F32)] * 2) * heads,
        compiler_params=_params(("parallel", "arbitrary")),
        name="gla_scan",
    )(proj, proj, proj, proj, proj, wg, bg, gnorm.reshape(1, GLA_DV), tri)


def _pad_heads(w, heads, width, padded):
    d = w.shape[0]
    return jnp.pad(w.reshape(d, heads, width), ((0, 0), (0, 0), (0, padded - width))).reshape(d, heads * padded)


def _gla_layer_seq(xn_proj_fn, w_in, w_gate_lr, b_gate, g_norm):
    sizes = np.cumsum([GLA_HEADS * GLA_DK, GLA_HEADS * GLA_DK, BRANCH_WIDTH, GLA_RANK, BRANCH_WIDTH, X_WIDTH])
    wq, wk, wv, wglr, wr, wxq, wxg = jnp.split(w_in, sizes.tolist(), axis=1)
    w_cat = jnp.concatenate([
        wv, wr, _pad_heads(wq, GLA_HEADS, GLA_DK, GLA_DKP), _pad_heads(wk, GLA_HEADS, GLA_DK, GLA_DKP), wxq, wxg,
        jnp.pad(wglr, ((0, 0), (0, GLA_NP - GLA_GLR - GLA_RANK)))], axis=1).astype(BF16)
    proj = xn_proj_fn(w_cat, GLA_TN)
    wg = jnp.pad(w_gate_lr.reshape(GLA_RANK, GLA_HEADS, GLA_DK).transpose(1, 0, 2),
                 ((0, 0), (0, GLA_RANKP - GLA_RANK), (0, GLA_DKP - GLA_DK))).astype(BF16)
    bg = jnp.pad(b_gate.reshape(GLA_HEADS, 1, GLA_DK), ((0, 0), (0, 0), (0, GLA_DKP - GLA_DK)))
    y_seq = _gla_scan(proj, wg, bg, g_norm, GLA_TB, GLA_HEADS_PER_STEP)
    return proj, y_seq, GLA_XQ // X_WIDTH, GLA_XG // X_WIDTH


def _cmp_mlp_body(x_ref, pe_ref, w1_ref, w2_ref, o_ref):
    x = (x_ref[...].astype(F32) + pe_ref[...]).astype(BF16)
    hid = _silu(jnp.dot(x, w1_ref[...], preferred_element_type=F32))
    o_ref[...] = jnp.dot(hid.astype(BF16), w2_ref[...], preferred_element_type=F32).astype(o_ref.dtype)


def _cmp_mlp(blocks, pe, w1, w2, tr):
    _, rows, width = blocks.shape
    return pl.pallas_call(
        _cmp_mlp_body,
        grid=(2, rows // tr),
        in_specs=[pl.BlockSpec((None, tr, width), lambda a, i: (a, i, 0)),
                  pl.BlockSpec((None, 1, width), lambda a, i: (a, 0, 0)),
                  pl.BlockSpec((None, width, CMP_HIDDEN), lambda a, i: (a, 0, 0)),
                  pl.BlockSpec((None, CMP_HIDDEN, HEAD_DIM), lambda a, i: (a, 0, 0))],
        out_specs=pl.BlockSpec((None, tr, HEAD_DIM), lambda a, i: (a, i, 0)),
        out_shape=jax.ShapeDtypeStruct((2, rows, HEAD_DIM), BF16),
        compiler_params=_params(("parallel", "parallel")),
        name="cmp_mlp",
    )(blocks, pe, w1, w2)


def _cmp_attn_body(q_ref, kct_ref, vca_ref, ovt_ref, o_ref, imp_ref, qs_sc, imp_sc, *, tq, widths):
    i = pl.program_id(0)
    n_tiles = pl.num_programs(0)
    for hh in range(NSA_HEADS):
        c = slice(hh * HEAD_DIM, (hh + 1) * HEAD_DIM)
        qs_sc[hh] = (q_ref[:, c].astype(F32) * (ATT_SCALE * LOG2E)).astype(BF16)

    def run(w):
        t = i * tq + lax.broadcasted_iota(jnp.int32, (tq, 1), 0)
        n = lax.broadcasted_iota(jnp.int32, (1, w), 1)
        bias = jnp.where(n * CMP_STRIDE + (CMP_LEN - 1) <= t, 0.0, NEG)
        has_key = t >= CMP_LEN - 1
        R = CMP_ROWS
        for g in range(NSA_GROUPS):
            for h in range(NSA_HPG):
                hh = g * NSA_HPG + h
                c = slice(hh * HEAD_DIM, (hh + 1) * HEAD_DIM)
                for r in range(tq // R):
                    rows = slice(r * R, (r + 1) * R)
                    s = jnp.dot(qs_sc[hh, rows, :], kct_ref[g, :, :w], preferred_element_type=F32) + bias[rows, :]
                    e = jnp.exp2(s - jnp.max(s, axis=-1, keepdims=True))
                    pv = jnp.dot(e.astype(BF16), vca_ref[g, :w, :], preferred_element_type=F32)
                    inv = jnp.where(has_key[rows, :], 1.0 / pv[:, HEAD_DIM:], 0.0)
                    o_ref[rows, c] = (pv[:, :HEAD_DIM] * inv).astype(o_ref.dtype)
                    p = e * _rep(inv, w)
                    if h == 0:
                        imp_sc[rows, :w] = p
                    else:
                        imp_sc[rows, :w] += p
            imp_ref[g] = _split_dot_nt(ovt_ref[:, :w], imp_sc[:, :w])

    per_class = n_tiles // len(widths)
    for cls, w in enumerate(widths):
        pl.when(i // per_class == cls)(functools.partial(run, w))


def _cmp_attn(proj, kct, vca, overlap_t, tq):
    t = proj.shape[0]
    nb, nc = overlap_t.shape
    classes = 4
    widths = tuple(nc * (c + 1) // classes for c in range(classes))
    assert (t // tq) % classes == 0 and all(w % min(LANES, nc // classes) == 0 for w in widths)
    return pl.pallas_call(
        functools.partial(_cmp_attn_body, tq=tq, widths=widths),
        grid=(t // tq,),
        in_specs=[pl.BlockSpec((tq, BRANCH_WIDTH), lambda i: (i, NSA_Q // BRANCH_WIDTH)),
                  pl.BlockSpec((NSA_GROUPS, HEAD_DIM, nc), lambda i: (0, 0, 0)),
                  pl.BlockSpec((NSA_GROUPS, nc, 2 * HEAD_DIM), lambda i: (0, 0, 0)),
                  pl.BlockSpec((nb, nc), lambda i: (0, 0))],
        out_specs=[pl.BlockSpec((tq, BRANCH_WIDTH), lambda i: (i, 0)),
                   pl.BlockSpec((NSA_GROUPS, nb, tq), lambda i: (0, 0, i))],
        out_shape=[jax.ShapeDtypeStruct((t, BRANCH_WIDTH), BF16),
                   jax.ShapeDtypeStruct((NSA_GROUPS, nb, t), F32)],
        scratch_shapes=[pltpu.VMEM((NSA_HEADS, tq, HEAD_DIM), BF16),
                        pltpu.VMEM((tq, nc), F32)],
        compiler_params=_params(("parallel",)),
        name="cmp_attn",
    )(proj, kct, vca, overlap_t)


def _topk_body(imp_ref, sel_ref, x_sc, *, tr, nb):
    t = pl.program_id(1) * tr + lax.broadcasted_iota(jnp.int32, (1, tr), 1)
    jb = lax.broadcasted_iota(jnp.int32, (nb, 1), 0)
    cur = jnp.right_shift(t, SLC_SHIFT)
    x = imp_ref[...]
    x = jnp.where((jb == 0) | (jb == cur) | (jb == cur - 1), FORCED, x)
    x_sc[...] = jnp.where(jb > cur, -FORCED, x)
    R = min(TOPK_LANES, tr)
    jbf = lax.broadcasted_iota(jnp.int32, (nb, R), 0).astype(F32)

    def pick(_, carry):
        for r in range(tr // R):
            cols = slice(r * R, (r + 1) * R)
            x = x_sc[:, cols]
            m = jnp.max(x, axis=0, keepdims=True)
            idx = jnp.min(jnp.where(x == m, jbf, float(nb)), axis=0, keepdims=True)
            x_sc[:, cols] = jnp.where(jbf == idx, PICKED, x)
        return carry

    lax.fori_loop(0, SLC_TOPN, pick, 0)
    sel_ref[...] = jnp.where(x_sc[...] < 0.5 * PICKED, 0.0, NEG).astype(sel_ref.dtype)


def _topk_select(imp_t, tr):
    groups, nb, t = imp_t.shape
    return pl.pallas_call(
        functools.partial(_topk_body, tr=tr, nb=nb),
        grid=(groups, t // tr),
        in_specs=[pl.BlockSpec((None, nb, tr), lambda g, i: (g, 0, i))],
        out_specs=pl.BlockSpec((None, nb, tr), lambda g, i: (g, 0, i)),
        out_shape=jax.ShapeDtypeStruct(imp_t.shape, BF16),
        scratch_shapes=[pltpu.VMEM((nb, tr), F32)],
        compiler_params=_params(("parallel", "parallel")),
        name="topk_select",
    )(imp_t)


def _win_attn_body(q_ref, kp_ref, kc_ref, vp_ref, vc_ref, o_ref, qs_sc, vpa_sc, vca_sc, *, tq):
    i = pl.program_id(1)
    R = WIN_ROWS
    for h in range(NSA_HPG):
        qs_sc[h] = (q_ref[:, h * HEAD_DIM:(h + 1) * HEAD_DIM].astype(F32) * (ATT_SCALE * LOG2E)).astype(BF16)
    for src, dst in ((vp_ref, vpa_sc), (vc_ref, vca_sc)):
        dst[:, :HEAD_DIM] = src[...]
        dst[:, HEAD_DIM:] = jnp.ones((tq, HEAD_DIM), BF16)
    col = lax.broadcasted_iota(jnp.int32, (1, R), 1)
    row = lax.broadcasted_iota(jnp.int32, (R, 1), 0)
    upto_query = jnp.where(col <= row, 0.0, NEG)
    inside_window = jnp.where(col > row, 0.0, NEG)

    def run(with_prev):
        for h in range(NSA_HPG):
            for r in range(tq // R):
                rows = slice(r * R, (r + 1) * R)
                n_cur = (r + 1) * R
                sc = jnp.dot(qs_sc[h, rows, :], kc_ref[:, :n_cur], preferred_element_type=F32)
                parts = ([sc[:, :n_cur - R]] if r > 0 else []) + [sc[:, n_cur - R:] + upto_query]
                if with_prev:
                    n_prev = tq - r * R
                    sp = jnp.dot(qs_sc[h, rows, :], kp_ref[:, r * R:], preferred_element_type=F32)
                    parts = [sp[:, :R] + inside_window] + ([sp[:, R:]] if n_prev > R else []) + parts
                s = parts[0] if len(parts) == 1 else jnp.concatenate(parts, axis=1)
                e = jnp.exp2(s - jnp.max(s, axis=-1, keepdims=True)).astype(BF16)
                if with_prev:
                    pv = (jnp.dot(e[:, :n_prev], vpa_sc[r * R:, :], preferred_element_type=F32)
                          + jnp.dot(e[:, n_prev:], vca_sc[:n_cur, :], preferred_element_type=F32))
                else:
                    pv = jnp.dot(e, vca_sc[:n_cur, :], preferred_element_type=F32)
                o_ref[rows, h * HEAD_DIM:(h + 1) * HEAD_DIM] = (pv[:, :HEAD_DIM] / pv[:, HEAD_DIM:]).astype(o_ref.dtype)

    pl.when(i == 0)(functools.partial(run, False))
    pl.when(i > 0)(functools.partial(run, True))


def _win_attn(proj, kt, tq):
    t = proj.shape[0]
    assert tq == WINDOW
    prev = lambda i: jnp.maximum(i - 1, 0)
    vb = NSA_VW // HEAD_DIM
    return pl.pallas_call(
        functools.partial(_win_attn_body, tq=tq),
        grid=(NSA_GROUPS, t // tq),
        in_specs=[pl.BlockSpec((tq, NSA_GW), lambda g, i: (i, g)),
                  pl.BlockSpec((HEAD_DIM, tq), lambda g, i: (NSA_KT_WIN + g, prev(i))),
                  pl.BlockSpec((HEAD_DIM, tq), lambda g, i: (NSA_KT_WIN + g, i)),
                  pl.BlockSpec((tq, HEAD_DIM), lambda g, i: (prev(i), vb + g)),
                  pl.BlockSpec((tq, HEAD_DIM), lambda g, i: (i, vb + g))],
        out_specs=pl.BlockSpec((tq, NSA_GW), lambda g, i: (i, g)),
        out_shape=jax.ShapeDtypeStruct((t, BRANCH_WIDTH), BF16),
        scratch_shapes=[pltpu.VMEM((NSA_HPG, tq, HEAD_DIM), BF16),
                        pltpu.VMEM((tq, 2 * HEAD_DIM), BF16),
                        pltpu.VMEM((tq, 2 * HEAD_DIM), BF16)],
        compiler_params=_params(("parallel", "parallel")),
        name="win_attn",
    )(proj, kt, kt, proj, proj)


def _slc_attn_body(it_ref, jt_ref, q_ref, k_ref, v_ref, sel_ref, ocmp_ref, owin_ref, gl_ref, r_ref, y_ref,
                   qa_sc, m_sc, acc_sc, ka_sc, va_sc, *, tq, tk, lw):
    step = pl.program_id(1)
    i = it_ref[step]
    j = jt_ref[step]

    @pl.when(j == 0)
    def _():
        for h in range(NSA_HPG):
            cs = slice(h * HEAD_DIM, (h + 1) * HEAD_DIM)
            qa_sc[h, :, :HEAD_DIM] = (q_ref[:, cs].astype(F32) * (ATT_SCALE * LOG2E)).astype(BF16)
        m_sc[...] = jnp.full_like(m_sc, NEG)
        acc_sc[...] = jnp.zeros_like(acc_sc)

    sel = sel_ref[...]
    for h in range(NSA_HPG):
        qa_sc[h, :, HEAD_DIM:] = sel
    key_blk = (j * (tk // SLC_LEN) + jnp.right_shift(lax.broadcasted_iota(jnp.int32, (1, tk), 1), SLC_SHIFT)) & (lw - 1)
    ka_sc[:HEAD_DIM, :] = k_ref[...]
    ka_sc[HEAD_DIM:, :] = jnp.where(lax.broadcasted_iota(jnp.int32, (lw, 1), 0) == key_blk, 1.0, 0.0).astype(BF16)
    va_sc[:, :HEAD_DIM] = v_ref[...]
    va_sc[:, HEAD_DIM:] = jnp.ones((tk, HEAD_DIM), BF16)

    def accumulate(diag):
        R = SLC_ROWS
        if diag:
            tri = jnp.where(lax.broadcasted_iota(jnp.int32, (1, R), 1) <= lax.broadcasted_iota(jnp.int32, (R, 1), 0),
                            0.0, NEG)
        for h in range(NSA_HPG):
            for r in range(tq // R):
                rows = slice(r * R, (r + 1) * R)
                ncols = (r + 1) * R if diag else tk
                s = jnp.dot(qa_sc[h, rows, :], ka_sc[:, :ncols], preferred_element_type=F32)
                if diag:
                    tail = s[:, ncols - R:] + tri
                    s = tail if r == 0 else jnp.concatenate([s[:, :ncols - R], tail], axis=1)
                m_old = m_sc[h, rows, :]
                m_new = jnp.maximum(m_old, jnp.max(s, axis=-1, keepdims=True))
                p = jnp.exp2(s - _rep(m_new, ncols))
                alpha = jnp.exp2(m_old - m_new)
                acc_sc[h, rows, :] = (_rep(alpha, 2 * HEAD_DIM) * acc_sc[h, rows, :]
                                      + jnp.dot(p.astype(BF16), va_sc[:ncols, :], preferred_element_type=F32))
                m_sc[h, rows, :] = m_new

    @pl.when(j < i)
    def _():
        accumulate(False)

    @pl.when(j == i)
    def _():
        accumulate(True)
        gates = jax.nn.sigmoid(gl_ref[...].astype(F32))
        for h in range(NSA_HPG):
            cs = slice(h * HEAD_DIM, (h + 1) * HEAD_DIM)
            o_slc = acc_sc[h, :, :HEAD_DIM] / acc_sc[h, :, HEAD_DIM:]
            g0 = gates[:, h * N_BRANCH + 0:h * N_BRANCH + 1]
            g1 = gates[:, h * N_BRANCH + 1:h * N_BRANCH + 2]
            g2 = gates[:, h * N_BRANCH + 2:h * N_BRANCH + 3]
            o = g0 * ocmp_ref[:, cs].astype(F32) + g1 * o_slc + g2 * owin_ref[:, cs].astype(F32)
            y_ref[:, cs] = (o * _silu(r_ref[:, cs].astype(F32))).astype(y_ref.dtype)


def _slc_attn(proj, kt, sel, o_cmp, o_win, tile):
    t = proj.shape[0]
    nb = sel.shape[1] // NSA_GROUPS
    lw = min(LANES, nb)
    assert lw & (lw - 1) == 0 and (tile // SLC_LEN) <= lw
    ni = t // tile
    it = np.concatenate([np.full(i + 1, i) for i in range(ni)]).astype(np.int32)
    jt = np.concatenate([np.arange(i + 1) for i in range(ni)]).astype(np.int32)
    blocks_per_tile = tile // SLC_LEN
    sel_idx = lambda g, s, it, jt: (it[s], g * (nb // lw) + (jt[s] * blocks_per_tile) // lw)
    row = lambda g, s, it, jt: (it[s], g)
    grid_spec = pltpu.PrefetchScalarGridSpec(
        num_scalar_prefetch=2,
        grid=(NSA_GROUPS, len(it)),
        in_specs=[pl.BlockSpec((tile, NSA_GW), row),
                  pl.BlockSpec((HEAD_DIM, tile), lambda g, s, it, jt: (NSA_KT_SLC + g, jt[s])),
                  pl.BlockSpec((tile, HEAD_DIM), lambda g, s, it, jt: (jt[s], NSA_VS // HEAD_DIM + g)),
                  pl.BlockSpec((tile, lw), sel_idx),
                  pl.BlockSpec((tile, NSA_GW), row),
                  pl.BlockSpec((tile, NSA_GW), row),
                  pl.BlockSpec((tile, HEAD_DIM), lambda g, s, it, jt: (it[s], NSA_GL // HEAD_DIM + g)),
                  pl.BlockSpec((tile, NSA_GW), lambda g, s, it, jt: (it[s], NSA_R // NSA_GW + g))],
        out_specs=pl.BlockSpec((tile, NSA_GW), row),
        scratch_shapes=[pltpu.VMEM((NSA_HPG, tile, HEAD_DIM + lw), BF16),
                        pltpu.VMEM((NSA_HPG, tile, HEAD_DIM), F32),
                        pltpu.VMEM((NSA_HPG, tile, 2 * HEAD_DIM), F32),
                        pltpu.VMEM((HEAD_DIM + lw, tile), BF16),
                        pltpu.VMEM((tile, 2 * HEAD_DIM), BF16)])
    return pl.pallas_call(
        functools.partial(_slc_attn_body, tq=tile, tk=tile, lw=lw),
        grid_spec=grid_spec,
        out_shape=jax.ShapeDtypeStruct((t, BRANCH_WIDTH), BF16),
        compiler_params=_params(("parallel", "arbitrary")),
        name="slc_attn",
    )(jnp.asarray(it), jnp.asarray(jt), proj, kt, proj, sel, o_cmp, o_win, proj, proj)


def _nsa_layer_seq(xn_proj_fn, w_in, pe_k, pe_v, wk1, wk2, wv1, wv2):
    t_sizes = [BRANCH_WIDTH] + [NSA_GROUPS * HEAD_DIM] * 6 + [NSA_HEADS * N_BRANCH, BRANCH_WIDTH, X_WIDTH]
    wq, wkc, wvc, wks, wvs, wkw, wvw, wgl, wr, wxq, wxg = jnp.split(w_in, np.cumsum(t_sizes).tolist(), axis=1)
    wgl = _pad_heads(wgl, NSA_GROUPS, NSA_HPG * N_BRANCH, HEAD_DIM)
    w_cat = jnp.concatenate([wq, wr, wkc, wvc, wvs, wvw, wxq, wxg, wgl], axis=1).astype(BF16)
    wt = jnp.concatenate([wks, wkw], axis=1).T.astype(BF16)
    proj, kt = xn_proj_fn(w_cat, NSA_TN, wt)
    t = proj.shape[0]

    half = CMP_STRIDE
    nc = t // half
    a = proj[:, NSA_KC:NSA_VS].reshape(nc, half, 2, NSA_GROUPS, HEAD_DIM)
    a = a.transpose(2, 3, 0, 1, 4).reshape(2, NSA_GROUPS, nc, half * HEAD_DIM)
    blocks = jnp.concatenate([a, jnp.roll(a, -1, axis=2)], axis=-1).reshape(2, NSA_GROUPS * nc, CMP_LEN * HEAD_DIM)
    pe = jnp.stack([pe_k, pe_v]).reshape(2, 1, CMP_LEN * HEAD_DIM)
    kv_cmp = _cmp_mlp(blocks, pe, jnp.stack([wk1, wv1]).astype(BF16), jnp.stack([wk2, wv2]).astype(BF16),
                      min(CMP_MLP_TR, NSA_GROUPS * nc))
    kv_cmp = kv_cmp.reshape(2, NSA_GROUPS, nc, HEAD_DIM)

    nb = t // SLC_LEN
    cstart = np.arange(nc) * CMP_STRIDE
    sstart = np.arange(nb) * SLC_LEN
    overlap = ((cstart[:, None] < sstart[None, :] + SLC_LEN) & (cstart[:, None] + CMP_LEN > sstart[None, :]))
    overlap[nc - 1] = False
    kct = kv_cmp[0].transpose(0, 2, 1)
    vca = jnp.concatenate([kv_cmp[1], jnp.ones_like(kv_cmp[1])], axis=-1)
    o_cmp, imp_t = _cmp_attn(proj, kct, vca, jnp.asarray(overlap.T.astype(np.float32), BF16), CMP_TQ)
    sel = _topk_select(imp_t, min(TOPK_TR, t)).transpose(2, 0, 1).reshape(t, NSA_GROUPS * nb)
    o_win = _win_attn(proj, kt, WINDOW)
    y_seq = _slc_attn(proj, kt, sel, o_cmp, o_win, SLC_TILE)
    return proj, y_seq, NSA_XQ // X_WIDTH, NSA_XG // X_WIDTH


def kernel(x, mem, ln_pre, ln_post, ln_mem, w_mem_kv, gla_w_in, gla_w_gate_lr, gla_b_gate, gla_norm, gla_w_out,
           nsa_w_in, nsa_pe_k, nsa_pe_v, nsa_wk1, nsa_wk2, nsa_wv1, nsa_wv2, nsa_w_out):
    batch, t, d = x.shape
    assert batch == 1 and d == D_MODEL and t % PROJ_TM == 0
    h = x.reshape(t, d)
    mem2 = mem.reshape(mem.shape[1], d)
    depth = ln_pre.shape[0]
    for i in range(depth):
        a = i // 2
        xn_proj_fn = lambda w, tn, wt=None, i=i, h=h: _norm_proj(h, ln_pre[i], w, PROJ_TM, tn, wt)
        kv = _norm_proj(mem2, ln_mem[i], w_mem_kv[i].astype(BF16), mem2.shape[0], 2 * X_WIDTH)
        if i % 2 == 0:
            proj, y_seq, xq_blk, xg_blk = _gla_layer_seq(xn_proj_fn, gla_w_in[a], gla_w_gate_lr[a], gla_b_gate[a],
                                                         gla_norm[a])
            w_out = gla_w_out[a]
        else:
            proj, y_seq, xq_blk, xg_blk = _nsa_layer_seq(xn_proj_fn, nsa_w_in[a], nsa_pe_k[a], nsa_pe_v[a],
                                                         nsa_wk1[a], nsa_wk2[a], nsa_wv1[a], nsa_wv2[a])
            w_out = nsa_w_out[a]
        y_mem = _mem_attn(proj, kv, xq_blk, xg_blk, MEM_TQ)
        h = _out_proj(y_seq, y_mem, w_out, ln_post[i], h, OUT_TM)
    return h.reshape(batch, t, d)
```

```python
import functools
import itertools

import numpy as np
import jax
import jax.numpy as jnp
from jax import lax
from jax.experimental import pallas as pl
from jax.experimental.pallas import tpu as pltpu

F32 = jnp.float32
BF16 = jnp.bfloat16

D_MODEL = 2048
HEAD_DIM = 128
EPS = 1e-6
X_HEADS = 4
X_WIDTH = X_HEADS * HEAD_DIM
BRANCH_WIDTH = D_MODEL - X_WIDTH
GLA_HEADS = 4
GLA_DV = BRANCH_WIDTH // GLA_HEADS
GLA_DK = GLA_DV // 2
GLA_DKP = 256
GLA_RANK = 16
GLA_RANKP = 128
GLA_TEMP = 16.0
GLA_CHUNK = 64
GLA_SUB = 16
NSA_HEADS = BRANCH_WIDTH // HEAD_DIM
NSA_GROUPS = 2
NSA_HPG = NSA_HEADS // NSA_GROUPS
NSA_GW = NSA_HPG * HEAD_DIM
N_BRANCH = 3
CMP_LEN = 32
CMP_STRIDE = 16
CMP_HIDDEN = 256
SLC_LEN = 64
SLC_SHIFT = SLC_LEN.bit_length() - 1
SLC_TOPN = 16
WINDOW = 512
FORCED = 1e4
NEG = -1e30
PICKED = -3e38
ATT_SCALE = HEAD_DIM ** -0.5
LOG2E = 1.4426950408889634

LANES = 128
VMEM_LIMIT = 56 * 1024 * 1024

PROJ_TM = 1024
GLA_TN, NSA_TN = 1280, 1792
GLA_TB = 512
GLA_HEADS_PER_STEP = 2
MEM_TQ = 512
OUT_TM = 256
CMP_MLP_TR = 512
CMP_TQ = 256
TOPK_TR = 1024
SLC_TILE = 1024
SLC_ROWS = 512
CMP_ROWS = 128
WIN_ROWS = 256
TOPK_LANES = 128

GLA_V, GLA_R, GLA_Q, GLA_K, GLA_XQ, GLA_XG, GLA_GLR, GLA_NP = 0, 1536, 3072, 4096, 5120, 5632, 6144, 6400
(NSA_Q, NSA_R, NSA_KC, NSA_VC, NSA_VS, NSA_VW, NSA_XQ, NSA_XG, NSA_GL, NSA_NP) = (
    0, 1536, 3072, 3328, 3584, 3840, 4096, 4608, 5120, 5376)
NSA_KT_SLC, NSA_KT_WIN = 0, NSA_GROUPS


def _params(sem):
    return pltpu.CompilerParams(dimension_semantics=sem, vmem_limit_bytes=VMEM_LIMIT)


def _silu(x):
    return x * jax.nn.sigmoid(x)


def _dot_nt(a, b):
    return lax.dot_general(a, b, (((1,), (1,)), ((), ())), preferred_element_type=F32)


def _dot_tn(a, b):
    return lax.dot_general(a, b, (((0,), (0,)), ((), ())), preferred_element_type=F32)


def _rep(x, width):
    return x[:, :width] if width < LANES else jnp.concatenate([x] * (width // LANES), axis=1)


def _split_dot_nt(w, x):
    hi = x.astype(BF16)
    lo = (x - hi.astype(F32)).astype(BF16)
    return _dot_nt(w, hi) + _dot_nt(w, lo)


def _split_dot_left(w, x):
    hi = x.astype(BF16)
    lo = (x - hi.astype(F32)).astype(BF16)
    return (jnp.dot(w, hi, preferred_element_type=F32) + jnp.dot(w, lo, preferred_element_type=F32))


def _norm_proj_body(x_ref, g_ref, w_ref, *rest, transposed):
    if transposed:
        wt_ref, o_ref, ot_ref, xn_ref = rest
    else:
        o_ref, xn_ref = rest

    @pl.when(pl.program_id(1) == 0)
    def _():
        x = x_ref[...]
        r = lax.rsqrt(jnp.mean(x * x, axis=-1, keepdims=True) + EPS)
        xn_ref[...] = ((x * r) * g_ref[...]).astype(BF16)
        if transposed:
            ot_ref[...] = _dot_nt(wt_ref[...], xn_ref[...]).astype(ot_ref.dtype)

    o_ref[...] = jnp.dot(xn_ref[...], w_ref[...], preferred_element_type=F32).astype(o_ref.dtype)


def _norm_proj(x, g, w, tm, tn, wt=None):
    rows, d = x.shape
    n = w.shape[1]
    in_specs = [pl.BlockSpec((tm, d), lambda i, j: (i, 0)),
                pl.BlockSpec((1, d), lambda i, j: (0, 0)),
                pl.BlockSpec((d, tn), lambda i, j: (0, j))]
    out_specs = pl.BlockSpec((tm, tn), lambda i, j: (i, j))
    out_shape = jax.ShapeDtypeStruct((rows, n), BF16)
    args = (x, g.reshape(1, d), w)
    if wt is not None:
        nt = wt.shape[0]
        in_specs.append(pl.BlockSpec((nt, d), lambda i, j: (0, 0)))
        out_specs = [out_specs, pl.BlockSpec((nt, tm), lambda i, j: (0, i))]
        out_shape = [out_shape, jax.ShapeDtypeStruct((nt, rows), BF16)]
        args = args + (wt,)
    return pl.pallas_call(
        functools.partial(_norm_proj_body, transposed=wt is not None),
        grid=(rows // tm, n // tn),
        in_specs=in_specs,
        out_specs=out_specs,
        out_shape=out_shape,
        scratch_shapes=[pltpu.VMEM((tm, d), BF16)],
        compiler_params=_params(("parallel", "arbitrary")),
        name="norm_proj",
    )(*args)


def _mem_attn_body(xq_ref, xg_ref, kv_ref, o_ref):
    for h in range(X_HEADS):
        c = slice(h * HEAD_DIM, (h + 1) * HEAD_DIM)
        k = kv_ref[:, c]
        v = kv_ref[:, X_WIDTH + h * HEAD_DIM: X_WIDTH + (h + 1) * HEAD_DIM]
        s = _dot_nt(xq_ref[:, c], k) * ATT_SCALE
        e = jnp.exp(s - jnp.max(s, axis=-1, keepdims=True))
        p = e / jnp.sum(e, axis=-1, keepdims=True)
        o = jnp.dot(p.astype(BF16), v, preferred_element_type=F32)
        o_ref[:, c] = (o * _silu(xg_ref[:, c].astype(F32))).astype(o_ref.dtype)


def _mem_attn(proj, kv, xq_blk, xg_blk, tq):
    t = proj.shape[0]
    m = kv.shape[0]
    return pl.pallas_call(
        _mem_attn_body,
        grid=(t // tq,),
        in_specs=[pl.BlockSpec((tq, X_WIDTH), lambda i: (i, xq_blk)),
                  pl.BlockSpec((tq, X_WIDTH), lambda i: (i, xg_blk)),
                  pl.BlockSpec((m, 2 * X_WIDTH), lambda i: (0, 0))],
        out_specs=pl.BlockSpec((tq, X_WIDTH), lambda i: (i, 0)),
        out_shape=jax.ShapeDtypeStruct((t, X_WIDTH), BF16),
        compiler_params=_params(("parallel",)),
        name="mem_attn",
    )(proj, proj, kv)


def _out_proj_body(ys_ref, ym_ref, ws_ref, wm_ref, g_ref, h_ref, o_ref):
    y = (jnp.dot(ys_ref[...], ws_ref[...], preferred_element_type=F32)
         + jnp.dot(ym_ref[...], wm_ref[...], preferred_element_type=F32))
    r = lax.rsqrt(jnp.mean(y * y, axis=-1, keepdims=True) + EPS)
    o_ref[...] = h_ref[...] + (y * r) * g_ref[...]


def _out_proj(y_seq, y_mem, w_out, g_post, h, tm):
    t, d = h.shape
    ws = w_out[:BRANCH_WIDTH].astype(BF16)
    wm = w_out[BRANCH_WIDTH:].astype(BF16)
    return pl.pallas_call(
        _out_proj_body,
        grid=(t // tm,),
        in_specs=[pl.BlockSpec((tm, BRANCH_WIDTH), lambda i: (i, 0)),
                  pl.BlockSpec((tm, X_WIDTH), lambda i: (i, 0)),
                  pl.BlockSpec((BRANCH_WIDTH, d), lambda i: (0, 0)),
                  pl.BlockSpec((X_WIDTH, d), lambda i: (0, 0)),
                  pl.BlockSpec((1, d), lambda i: (0, 0)),
                  pl.BlockSpec((tm, d), lambda i: (i, 0))],
        out_specs=pl.BlockSpec((tm, d), lambda i: (i, 0)),
        out_shape=jax.ShapeDtypeStruct((t, d), F32),
        compiler_params=_params(("parallel",)),
        name="out_proj",
    )(y_seq, y_mem, ws, wm, g_post.reshape(1, d), h)


def _gla_body(q_ref, k_ref, v_ref, r_ref, glr_ref, wg_ref, bg_ref, gn_ref, tri_ref, y_ref, *scratch, chunks, heads):
    C = GLA_CHUNK
    SB = GLA_SUB
    per_head = len(scratch) // heads
    state = [scratch[hh * per_head] for hh in range(heads)]
    bufs = [[scratch[hh * per_head + 1 + 4 * par: hh * per_head + 5 + 4 * par] for par in range(2)]
            for hh in range(heads)]

    @pl.when(pl.program_id(1) == 0)
    def _():
        for st_ref in state:
            st_ref[...] = jnp.zeros_like(st_ref)

    def free_part(hh, c, par):
        rows = pl.ds(pl.multiple_of(c * C, C), C)
        kc = slice(hh * GLA_DKP, (hh + 1) * GLA_DKP)
        vc = slice(hh * GLA_DV, (hh + 1) * GLA_DV)
        b_h, q_h, k_h, o_h = bufs[hh][par]
        gp = jnp.dot(glr_ref[rows, :], wg_ref[hh], preferred_element_type=F32) + bg_ref[hh]
        logg = (jnp.minimum(gp, 0.0) - jnp.log1p(jnp.exp(-jnp.abs(gp)))) * (LOG2E / GLA_TEMP)
        b_h[...] = _split_dot_left(tri_ref[...], logg)
        q_h[...] = q_ref[rows, kc].astype(F32) * (GLA_DK ** -0.5)
        k_h[...] = k_ref[rows, kc].astype(F32)
        yield

        qs, ks = [], []
        for j in range(C // SB - 1):
            lo, hi = j * SB, (j + 1) * SB
            beta = b_h[hi:hi + 1, :]
            qt = (q_h[hi:, :] * jnp.exp2(b_h[hi:, :] - beta)).astype(BF16)
            kt = (k_h[lo:hi, :] * jnp.exp2(beta - b_h[lo:hi, :])).astype(BF16)
            qs.append(jnp.concatenate([jnp.zeros((hi, GLA_DKP), BF16), qt], axis=0))
            ks.append(jnp.concatenate(([jnp.zeros((lo, GLA_DKP), BF16)] if lo else []) + [kt]
                                      + [jnp.zeros((C - hi, GLA_DKP), BF16)], axis=0))
        coef_between = _dot_nt(jnp.concatenate(qs, axis=1), jnp.concatenate(ks, axis=1))
        yield

        sub8 = lax.broadcasted_iota(jnp.int32, (8, 1), 0)
        lane = lax.broadcasted_iota(jnp.int32, (1, LANES), 1)
        tiles = []
        for blk in range(C // SB):
            base = blk * SB
            coef = [jnp.zeros((8, LANES), F32) for _ in range(SB // 8)]
            for s in range(SB):
                row = base + s
                bs = b_h[row:row + 1, :]
                ks_row = k_h[row:row + 1, :]
                for tix in range(s // 8, SB // 8):
                    r0 = base + tix * 8
                    e = jnp.exp2(b_h[r0:r0 + 8, :] - bs)
                    if tix == s // 8:
                        e = jnp.where(sub8 >= s % 8, e, 0.0)
                    a = jnp.sum(q_h[r0:r0 + 8, :] * ks_row * e, axis=-1, keepdims=True)
                    coef[tix] = jnp.where(lane == row, a, coef[tix])
            tiles += coef
            yield
        coef_all = (coef_between + jnp.concatenate(tiles, axis=0)[:, :C]).astype(BF16)
        o_h[...] = jnp.dot(coef_all, v_ref[rows, vc], preferred_element_type=F32)

    def state_part(hh, c, par):
        rows = pl.ds(pl.multiple_of(c * C, C), C)
        vc = slice(hh * GLA_DV, (hh + 1) * GLA_DV)
        b_h, q_h, k_h, o_h = bufs[hh][par]
        st_ref = state[hh]
        b = b_h[...]
        st = st_ref[...]
        o = o_h[...] + _dot_nt((q_h[...] * jnp.exp2(b)).astype(BF16), st.astype(BF16))
        yield
        bl = b[C - 1:C, :]
        kd = (k_h[...] * jnp.exp2(bl - b)).astype(BF16)
        st_ref[...] = jnp.exp2(bl) * st + _dot_tn(v_ref[rows, vc], kd)
        yield
        on = (o * lax.rsqrt(jnp.mean(o * o, axis=-1, keepdims=True) + EPS)) * gn_ref[...]
        y_ref[rows, vc] = (on * _silu(r_ref[rows, vc].astype(F32))).astype(y_ref.dtype)

    def run(*gens):
        for _ in itertools.zip_longest(*gens):
            pass

    def pair(c, with_next):
        run(*[state_part(hh, c, 0) for hh in range(heads)], *[free_part(hh, c + 1, 1) for hh in range(heads)])
        run(*[state_part(hh, c + 1, 1) for hh in range(heads)],
            *([free_part(hh, c + 2, 0) for hh in range(heads)] if with_next else []))

    run(*[free_part(hh, 0, 0) for hh in range(heads)])

    def body(c2, carry):
        pair(2 * c2, True)
        return carry

    lax.fori_loop(0, chunks // 2 - 1, body, 0)
    pair(chunks - 2, False)


def _gla_scan(proj, wg, bg, gnorm, tb, heads):
    t = proj.shape[0]
    C = GLA_CHUNK
    tri = jnp.asarray(np.tril(np.ones((C, C), np.float32)), BF16)
    kw, vw = heads * GLA_DKP, heads * GLA_DV
    qb, kb, vb, rb, gb = GLA_Q // kw, GLA_K // kw, GLA_V // vw, GLA_R // vw, GLA_GLR // GLA_RANKP
    return pl.pallas_call(
        functools.partial(_gla_body, chunks=tb // C, heads=heads),
        grid=(GLA_HEADS // heads, t // tb),
        in_specs=[pl.BlockSpec((tb, kw), lambda h, n: (n, qb + h)),
                  pl.BlockSpec((tb, kw), lambda h, n: (n, kb + h)),
                  pl.BlockSpec((tb, vw), lambda h, n: (n, vb + h)),
                  pl.BlockSpec((tb, vw), lambda h, n: (n, rb + h)),
                  pl.BlockSpec((tb, GLA_RANKP), lambda h, n: (n, gb)),
                  pl.BlockSpec((heads, GLA_RANKP, GLA_DKP), lambda h, n: (h, 0, 0)),
                  pl.BlockSpec((heads, 1, GLA_DKP), lambda h, n: (h, 0, 0)),
                  pl.BlockSpec((1, GLA_DV), lambda h, n: (0, 0)),
                  pl.BlockSpec((C, C), lambda h, n: (0, 0))],
        out_specs=pl.BlockSpec((tb, vw), lambda h, n: (n, h)),
        out_shape=jax.ShapeDtypeStruct((t, BRANCH_WIDTH), BF16),
        scratch_shapes=([pltpu.VMEM((GLA_DV, GLA_DKP), F32)]
                        + [pltpu.VMEM((C, GLA_DKP), F32), pltpu.VMEM((C, GLA_DKP), F32),
                           pltpu.VMEM((C, GLA_DKP), F32), pltpu.VMEM((C, GLA_DV), F32)] * 2) * heads,
        compiler_params=_params(("parallel", "arbitrary")),
        name="gla_scan",
    )(proj, proj, proj, proj, proj, wg, bg, gnorm.reshape(1, GLA_DV), tri)


def _pad_heads(w, heads, width, padded):
    d = w.shape[0]
    return jnp.pad(w.reshape(d, heads, width), ((0, 0), (0, 0), (0, padded - width))).reshape(d, heads * padded)


def _gla_layer_seq(xn_proj_fn, w_in, w_gate_lr, b_gate, g_norm):
    sizes = np.cumsum([GLA_HEADS * GLA_DK, GLA_HEADS * GLA_DK, BRANCH_WIDTH, GLA_RANK, BRANCH_WIDTH, X_WIDTH])
    wq, wk, wv, wglr, wr, wxq, wxg = jnp.split(w_in, sizes.tolist(), axis=1)
    w_cat = jnp.concatenate([
        wv, wr, _pad_heads(wq, GLA_HEADS, GLA_DK, GLA_DKP), _pad_heads(wk, GLA_HEADS, GLA_DK, GLA_DKP), wxq, wxg,
        jnp.pad(wglr, ((0, 0), (0, GLA_NP - GLA_GLR - GLA_RANK)))], axis=1).astype(BF16)
    proj = xn_proj_fn(w_cat, GLA_TN)
    wg = jnp.pad(w_gate_lr.reshape(GLA_RANK, GLA_HEADS, GLA_DK).transpose(1, 0, 2),
                 ((0, 0), (0, GLA_RANKP - GLA_RANK), (0, GLA_DKP - GLA_DK))).astype(BF16)
    bg = jnp.pad(b_gate.reshape(GLA_HEADS, 1, GLA_DK), ((0, 0), (0, 0), (0, GLA_DKP - GLA_DK)))
    y_seq = _gla_scan(proj, wg, bg, g_norm, GLA_TB, GLA_HEADS_PER_STEP)
    return proj, y_seq, GLA_XQ // X_WIDTH, GLA_XG // X_WIDTH


def _cmp_mlp_body(x_ref, pe_ref, w1_ref, w2_ref, o_ref):
    x = (x_ref[...].astype(F32) + pe_ref[...]).astype(BF16)
    hid = _silu(jnp.dot(x, w1_ref[...], preferred_element_type=F32))
    o_ref[...] = jnp.dot(hid.astype(BF16), w2_ref[...], preferred_element_type=F32).astype(o_ref.dtype)


def _cmp_mlp(blocks, pe, w1, w2, tr):
    _, rows, width = blocks.shape
    return pl.pallas_call(
        _cmp_mlp_body,
        grid=(2, rows // tr),
        in_specs=[pl.BlockSpec((None, tr, width), lambda a, i: (a, i, 0)),
                  pl.BlockSpec((None, 1, width), lambda a, i: (a, 0, 0)),
                  pl.BlockSpec((None, width, CMP_HIDDEN), lambda a, i: (a, 0, 0)),
                  pl.BlockSpec((None, CMP_HIDDEN, HEAD_DIM), lambda a, i: (a, 0, 0))],
        out_specs=pl.BlockSpec((None, tr, HEAD_DIM), lambda a, i: (a, i, 0)),
        out_shape=jax.ShapeDtypeStruct((2, rows, HEAD_DIM), BF16),
        compiler_params=_params(("parallel", "parallel")),
        name="cmp_mlp",
    )(blocks, pe, w1, w2)


def _cmp_attn_body(q_ref, kct_ref, vca_ref, ovt_ref, o_ref, imp_ref, qs_sc, imp_sc, *, tq, widths):
    i = pl.program_id(0)
    n_tiles = pl.num_programs(0)
    for hh in range(NSA_HEADS):
        c = slice(hh * HEAD_DIM, (hh + 1) * HEAD_DIM)
        qs_sc[hh] = (q_ref[:, c].astype(F32) * (ATT_SCALE * LOG2E)).astype(BF16)

    def run(w):
        t = i * tq + lax.broadcasted_iota(jnp.int32, (tq, 1), 0)
        n = lax.broadcasted_iota(jnp.int32, (1, w), 1)
        bias = jnp.where(n * CMP_STRIDE + (CMP_LEN - 1) <= t, 0.0, NEG)
        has_key = t >= CMP_LEN - 1
        R = CMP_ROWS
        for g in range(NSA_GROUPS):
            for h in range(NSA_HPG):
                hh = g * NSA_HPG + h
                c = slice(hh * HEAD_DIM, (hh + 1) * HEAD_DIM)
                for r in range(tq // R):
                    rows = slice(r * R, (r + 1) * R)
                    s = jnp.dot(qs_sc[hh, rows, :], kct_ref[g, :, :w], preferred_element_type=F32) + bias[rows, :]
                    e = jnp.exp2(s - jnp.max(s, axis=-1, keepdims=True))
                    pv = jnp.dot(e.astype(BF16), vca_ref[g, :w, :], preferred_element_type=F32)
                    inv = jnp.where(has_key[rows, :], 1.0 / pv[:, HEAD_DIM:], 0.0)
                    o_ref[rows, c] = (pv[:, :HEAD_DIM] * inv).astype(o_ref.dtype)
                    p = e * _rep(inv, w)
                    if h == 0:
                        imp_sc[rows, :w] = p
                    else:
                        imp_sc[rows, :w] += p
            imp_ref[g] = _split_dot_nt(ovt_ref[:, :w], imp_sc[:, :w])

    per_class = n_tiles // len(widths)
    for cls, w in enumerate(widths):
        pl.when(i // per_class == cls)(functools.partial(run, w))


def _cmp_attn(proj, kct, vca, overlap_t, tq):
    t = proj.shape[0]
    nb, nc = overlap_t.shape
    classes = 4
    widths = tuple(nc * (c + 1) // classes for c in range(classes))
    assert (t // tq) % classes == 0 and all(w % min(LANES, nc // classes) == 0 for w in widths)
    return pl.pallas_call(
        functools.partial(_cmp_attn_body, tq=tq, widths=widths),
        grid=(t // tq,),
        in_specs=[pl.BlockSpec((tq, BRANCH_WIDTH), lambda i: (i, NSA_Q // BRANCH_WIDTH)),
                  pl.BlockSpec((NSA_GROUPS, HEAD_DIM, nc), lambda i: (0, 0, 0)),
                  pl.BlockSpec((NSA_GROUPS, nc, 2 * HEAD_DIM), lambda i: (0, 0, 0)),
                  pl.BlockSpec((nb, nc), lambda i: (0, 0))],
        out_specs=[pl.BlockSpec((tq, BRANCH_WIDTH), lambda i: (i, 0)),
                   pl.BlockSpec((NSA_GROUPS, nb, tq), lambda i: (0, 0, i))],
        out_shape=[jax.ShapeDtypeStruct((t, BRANCH_WIDTH), BF16),
                   jax.ShapeDtypeStruct((NSA_GROUPS, nb, t), F32)],
        scratch_shapes=[pltpu.VMEM((NSA_HEADS, tq, HEAD_DIM), BF16),
                        pltpu.VMEM((tq, nc), F32)],
        compiler_params=_params(("parallel",)),
        name="cmp_attn",
    )(proj, kct, vca, overlap_t)


def _topk_body(imp_ref, sel_ref, x_sc, *, tr, nb):
    t = pl.program_id(1) * tr + lax.broadcasted_iota(jnp.int32, (1, tr), 1)
    jb = lax.broadcasted_iota(jnp.int32, (nb, 1), 0)
    cur = jnp.right_shift(t, SLC_SHIFT)
    x = imp_ref[...]
    x = jnp.where((jb == 0) | (jb == cur) | (jb == cur - 1), FORCED, x)
    x_sc[...] = jnp.where(jb > cur, -FORCED, x)
    R = min(TOPK_LANES, tr)
    jbf = lax.broadcasted_iota(jnp.int32, (nb, R), 0).astype(F32)

    def pick(_, carry):
        for r in range(tr // R):
            cols = slice(r * R, (r + 1) * R)
            x = x_sc[:, cols]
            m = jnp.max(x, axis=0, keepdims=True)
            idx = jnp.min(jnp.where(x == m, jbf, float(nb)), axis=0, keepdims=True)
            x_sc[:, cols] = jnp.where(jbf == idx, PICKED, x)
        return carry

    lax.fori_loop(0, SLC_TOPN, pick, 0)
    sel_ref[...] = jnp.where(x_sc[...] < 0.5 * PICKED, 0.0, NEG).astype(sel_ref.dtype)


def _topk_select(imp_t, tr):
    groups, nb, t = imp_t.shape
    return pl.pallas_call(
        functools.partial(_topk_body, tr=tr, nb=nb),
        grid=(groups, t // tr),
        in_specs=[pl.BlockSpec((None, nb, tr), lambda g, i: (g, 0, i))],
        out_specs=pl.BlockSpec((None, nb, tr), lambda g, i: (g, 0, i)),
        out_shape=jax.ShapeDtypeStruct(imp_t.shape, BF16),
        scratch_shapes=[pltpu.VMEM((nb, tr), F32)],
        compiler_params=_params(("parallel", "parallel")),
        name="topk_select",
    )(imp_t)


def _win_attn_body(q_ref, kp_ref, kc_ref, vp_ref, vc_ref, o_ref, qs_sc, vpa_sc, vca_sc, *, tq):
    i = pl.program_id(1)
    R = WIN_ROWS
    for h in range(NSA_HPG):
        qs_sc[h] = (q_ref[:, h * HEAD_DIM:(h + 1) * HEAD_DIM].astype(F32) * (ATT_SCALE * LOG2E)).astype(BF16)
    for src, dst in ((vp_ref, vpa_sc), (vc_ref, vca_sc)):
        dst[:, :HEAD_DIM] = src[...]
        dst[:, HEAD_DIM:] = jnp.ones((tq, HEAD_DIM), BF16)
    col = lax.broadcasted_iota(jnp.int32, (1, R), 1)
    row = lax.broadcasted_iota(jnp.int32, (R, 1), 0)
    upto_query = jnp.where(col <= row, 0.0, NEG)
    inside_window = jnp.where(col > row, 0.0, NEG)

    def run(with_prev):
        for h in range(NSA_HPG):
            for r in range(tq // R):
                rows = slice(r * R, (r + 1) * R)
                n_cur = (r + 1) * R
                sc = jnp.dot(qs_sc[h, rows, :], kc_ref[:, :n_cur], preferred_element_type=F32)
                parts = ([sc[:, :n_cur - R]] if r > 0 else []) + [sc[:, n_cur - R:] + upto_query]
                if with_prev:
                    n_prev = tq - r * R
                    sp = jnp.dot(qs_sc[h, rows, :], kp_ref[:, r * R:], preferred_element_type=F32)
                    parts = [sp[:, :R] + inside_window] + ([sp[:, R:]] if n_prev > R else []) + parts
                s = parts[0] if len(parts) == 1 else jnp.concatenate(parts, axis=1)
                e = jnp.exp2(s - jnp.max(s, axis=-1, keepdims=True)).astype(BF16)
                if with_prev:
                    pv = (jnp.dot(e[:, :n_prev], vpa_sc[r * R:, :], preferred_element_type=F32)
                          + jnp.dot(e[:, n_prev:], vca_sc[:n_cur, :], preferred_element_type=F32))
                else:
                    pv = jnp.dot(e, vca_sc[:n_cur, :], preferred_element_type=F32)
                o_ref[rows, h * HEAD_DIM:(h + 1) * HEAD_DIM] = (pv[:, :HEAD_DIM] / pv[:, HEAD_DIM:]).astype(o_ref.dtype)

    pl.when(i == 0)(functools.partial(run, False))
    pl.when(i > 0)(functools.partial(run, True))


def _win_attn(proj, kt, tq):
    t = proj.shape[0]
    assert tq == WINDOW
    prev = lambda i: jnp.maximum(i - 1, 0)
    vb = NSA_VW // HEAD_DIM
    return pl.pallas_call(
        functools.partial(_win_attn_body, tq=tq),
        grid=(NSA_GROUPS, t // tq),
        in_specs=[pl.BlockSpec((tq, NSA_GW), lambda g, i: (i, g)),
                  pl.BlockSpec((HEAD_DIM, tq), lambda g, i: (NSA_KT_WIN + g, prev(i))),
                  pl.BlockSpec((HEAD_DIM, tq), lambda g, i: (NSA_KT_WIN + g, i)),
                  pl.BlockSpec((tq, HEAD_DIM), lambda g, i: (prev(i), vb + g)),
                  pl.BlockSpec((tq, HEAD_DIM), lambda g, i: (i, vb + g))],
        out_specs=pl.BlockSpec((tq, NSA_GW), lambda g, i: (i, g)),
        out_shape=jax.ShapeDtypeStruct((t, BRANCH_WIDTH), BF16),
        scratch_shapes=[pltpu.VMEM((NSA_HPG, tq, HEAD_DIM), BF16),
                        pltpu.VMEM((tq, 2 * HEAD_DIM), BF16),
                        pltpu.VMEM((tq, 2 * HEAD_DIM), BF16)],
        compiler_params=_params(("parallel", "parallel")),
        name="win_attn",
    )(proj, kt, kt, proj, proj)


def _slc_attn_body(it_ref, jt_ref, q_ref, k_ref, v_ref, sel_ref, ocmp_ref, owin_ref, gl_ref, r_ref, y_ref,
                   qa_sc, m_sc, acc_sc, ka_sc, va_sc, *, tq, tk, lw):
    step = pl.program_id(1)
    i = it_ref[step]
    j = jt_ref[step]

    @pl.when(j == 0)
    def _():
        for h in range(NSA_HPG):
            cs = slice(h * HEAD_DIM, (h + 1) * HEAD_DIM)
            qa_sc[h, :, :HEAD_DIM] = (q_ref[:, cs].astype(F32) * (ATT_SCALE * LOG2E)).astype(BF16)
        m_sc[...] = jnp.full_like(m_sc, NEG)
        acc_sc[...] = jnp.zeros_like(acc_sc)

    sel = sel_ref[...]
    for h in range(NSA_HPG):
        qa_sc[h, :, HEAD_DIM:] = sel
    key_blk = (j * (tk // SLC_LEN) + jnp.right_shift(lax.broadcasted_iota(jnp.int32, (1, tk), 1), SLC_SHIFT)) & (lw - 1)
    ka_sc[:HEAD_DIM, :] = k_ref[...]
    ka_sc[HEAD_DIM:, :] = jnp.where(lax.broadcasted_iota(jnp.int32, (lw, 1), 0) == key_blk, 1.0, 0.0).astype(BF16)
    va_sc[:, :HEAD_DIM] = v_ref[...]
    va_sc[:, HEAD_DIM:] = jnp.ones((tk, HEAD_DIM), BF16)

    def accumulate(diag):
        R = SLC_ROWS
        if diag:
            tri = jnp.where(lax.broadcasted_iota(jnp.int32, (1, R), 1) <= lax.broadcasted_iota(jnp.int32, (R, 1), 0),
                            0.0, NEG)
        for h in range(NSA_HPG):
            for r in range(tq // R):
                rows = slice(r * R, (r + 1) * R)
                ncols = (r + 1) * R if diag else tk
                s = jnp.dot(qa_sc[h, rows, :], ka_sc[:, :ncols], preferred_element_type=F32)
                if diag:
                    tail = s[:, ncols - R:] + tri
                    s = tail if r == 0 else jnp.concatenate([s[:, :ncols - R], tail], axis=1)
                m_old = m_sc[h, rows, :]
                m_new = jnp.maximum(m_old, jnp.max(s, axis=-1, keepdims=True))
                p = jnp.exp2(s - _rep(m_new, ncols))
                alpha = jnp.exp2(m_old - m_new)
                acc_sc[h, rows, :] = (_rep(alpha, 2 * HEAD_DIM) * acc_sc[h, rows, :]
                                      + jnp.dot(p.astype(BF16), va_sc[:ncols, :], preferred_element_type=F32))
                m_sc[h, rows, :] = m_new

    @pl.when(j < i)
    def _():
        accumulate(False)

    @pl.when(j == i)
    def _():
        accumulate(True)
        gates = jax.nn.sigmoid(gl_ref[...].astype(F32))
        for h in range(NSA_HPG):
            cs = slice(h * HEAD_DIM, (h + 1) * HEAD_DIM)
            o_slc = acc_sc[h, :, :HEAD_DIM] / acc_sc[h, :, HEAD_DIM:]
            g0 = gates[:, h * N_BRANCH + 0:h * N_BRANCH + 1]
            g1 = gates[:, h * N_BRANCH + 1:h * N_BRANCH + 2]
            g2 = gates[:, h * N_BRANCH + 2:h * N_BRANCH + 3]
            o = g0 * ocmp_ref[:, cs].astype(F32) + g1 * o_slc + g2 * owin_ref[:, cs].astype(F32)
            y_ref[:, cs] = (o * _silu(r_ref[:, cs].astype(F32))).astype(y_ref.dtype)


def _slc_attn(proj, kt, sel, o_cmp, o_win, tile):
    t = proj.shape[0]
    nb = sel.shape[1] // NSA_GROUPS
    lw = min(LANES, nb)
    assert lw & (lw - 1) == 0 and (tile // SLC_LEN) <= lw
    ni = t // tile
    it = np.concatenate([np.full(i + 1, i) for i in range(ni)]).astype(np.int32)
    jt = np.concatenate([np.arange(i + 1) for i in range(ni)]).astype(np.int32)
    blocks_per_tile = tile // SLC_LEN
    sel_idx = lambda g, s, it, jt: (it[s], g * (nb // lw) + (jt[s] * blocks_per_tile) // lw)
    row = lambda g, s, it, jt: (it[s], g)
    grid_spec = pltpu.PrefetchScalarGridSpec(
        num_scalar_prefetch=2,
        grid=(NSA_GROUPS, len(it)),
        in_specs=[pl.BlockSpec((tile, NSA_GW), row),
                  pl.BlockSpec((HEAD_DIM, tile), lambda g, s, it, jt: (NSA_KT_SLC + g, jt[s])),
                  pl.BlockSpec((tile, HEAD_DIM), lambda g, s, it, jt: (jt[s], NSA_VS // HEAD_DIM + g)),
                  pl.BlockSpec((tile, lw), sel_idx),
                  pl.BlockSpec((tile, NSA_GW), row),
                  pl.BlockSpec((tile, NSA_GW), row),
                  pl.BlockSpec((tile, HEAD_DIM), lambda g, s, it, jt: (it[s], NSA_GL // HEAD_DIM + g)),
                  pl.BlockSpec((tile, NSA_GW), lambda g, s, it, jt: (it[s], NSA_R // NSA_GW + g))],
        out_specs=pl.BlockSpec((tile, NSA_GW), row),
        scratch_shapes=[pltpu.VMEM((NSA_HPG, tile, HEAD_DIM + lw), BF16),
                        pltpu.VMEM((NSA_HPG, tile, HEAD_DIM), F32),
                        pltpu.VMEM((NSA_HPG, tile, 2 * HEAD_DIM), F32),
                        pltpu.VMEM((HEAD_DIM + lw, tile), BF16),
                        pltpu.VMEM((tile, 2 * HEAD_DIM), BF16)])
    return pl.pallas_call(
        functools.partial(_slc_attn_body, tq=tile, tk=tile, lw=lw),
        grid_spec=grid_spec,
        out_shape=jax.ShapeDtypeStruct((t, BRANCH_WIDTH), BF16),
        compiler_params=_params(("parallel", "arbitrary")),
        name="slc_attn",
    )(jnp.asarray(it), jnp.asarray(jt), proj, kt, proj, sel, o_cmp, o_win, proj, proj)


def _nsa_layer_seq(xn_proj_fn, w_in, pe_k, pe_v, wk1, wk2, wv1, wv2):
    t_sizes = [BRANCH_WIDTH] + [NSA_GROUPS * HEAD_DIM] * 6 + [NSA_HEADS * N_BRANCH, BRANCH_WIDTH, X_WIDTH]
    wq, wkc, wvc, wks, wvs, wkw, wvw, wgl, wr, wxq, wxg = jnp.split(w_in, np.cumsum(t_sizes).tolist(), axis=1)
    wgl = _pad_heads(wgl, NSA_GROUPS, NSA_HPG * N_BRANCH, HEAD_DIM)
    w_cat = jnp.concatenate([wq, wr, wkc, wvc, wvs, wvw, wxq, wxg, wgl], axis=1).astype(BF16)
    wt = jnp.concatenate([wks, wkw], axis=1).T.astype(BF16)
    proj, kt = xn_proj_fn(w_cat, NSA_TN, wt)
    t = proj.shape[0]

    half = CMP_STRIDE
    nc = t // half
    a = proj[:, NSA_KC:NSA_VS].reshape(nc, half, 2, NSA_GROUPS, HEAD_DIM)
    a = a.transpose(2, 3, 0, 1, 4).reshape(2, NSA_GROUPS, nc, half * HEAD_DIM)
    blocks = jnp.concatenate([a, jnp.roll(a, -1, axis=2)], axis=-1).reshape(2, NSA_GROUPS * nc, CMP_LEN * HEAD_DIM)
    pe = jnp.stack([pe_k, pe_v]).reshape(2, 1, CMP_LEN * HEAD_DIM)
    kv_cmp = _cmp_mlp(blocks, pe, jnp.stack([wk1, wv1]).astype(BF16), jnp.stack([wk2, wv2]).astype(BF16),
                      min(CMP_MLP_TR, NSA_GROUPS * nc))
    kv_cmp = kv_cmp.reshape(2, NSA_GROUPS, nc, HEAD_DIM)

    nb = t // SLC_LEN
    cstart = np.arange(nc) * CMP_STRIDE
    sstart = np.arange(nb) * SLC_LEN
    overlap = ((cstart[:, None] < sstart[None, :] + SLC_LEN) & (cstart[:, None] + CMP_LEN > sstart[None, :]))
    overlap[nc - 1] = False
    kct = kv_cmp[0].transpose(0, 2, 1)
    vca = jnp.concatenate([kv_cmp[1], jnp.ones_like(kv_cmp[1])], axis=-1)
    o_cmp, imp_t = _cmp_attn(proj, kct, vca, jnp.asarray(overlap.T.astype(np.float32), BF16), CMP_TQ)
    sel = _topk_select(imp_t, min(TOPK_TR, t)).transpose(2, 0, 1).reshape(t, NSA_GROUPS * nb)
    o_win = _win_attn(proj, kt, WINDOW)
    y_seq = _slc_attn(proj, kt, sel, o_cmp, o_win, SLC_TILE)
    return proj, y_seq, NSA_XQ // X_WIDTH, NSA_XG // X_WIDTH


def kernel(x, mem, ln_pre, ln_post, ln_mem, w_mem_kv, gla_w_in, gla_w_gate_lr, gla_b_gate, gla_norm, gla_w_out,
           nsa_w_in, nsa_pe_k, nsa_pe_v, nsa_wk1, nsa_wk2, nsa_wv1, nsa_wv2, nsa_w_out):
    batch, t, d = x.shape
    assert batch == 1 and d == D_MODEL and t % PROJ_TM == 0
    h = x.reshape(t, d)
    mem2 = mem.reshape(mem.shape[1], d)
    depth = ln_pre.shape[0]
    for i in range(depth):
        a = i // 2
        xn_proj_fn = lambda w, tn, wt=None, i=i, h=h: _norm_proj(h, ln_pre[i], w, PROJ_TM, tn, wt)
        kv = _norm_proj(mem2, ln_mem[i], w_mem_kv[i].astype(BF16), mem2.shape[0], 2 * X_WIDTH)
        if i % 2 == 0:
            proj, y_seq, xq_blk, xg_blk = _gla_layer_seq(xn_proj_fn, gla_w_in[a], gla_w_gate_lr[a], gla_b_gate[a],
                                                         gla_norm[a])
            w_out = gla_w_out[a]
        else:
            proj, y_seq, xq_blk, xg_blk = _nsa_layer_seq(xn_proj_fn, nsa_w_in[a], nsa_pe_k[a], nsa_pe_v[a],
                                                         nsa_wk1[a], nsa_wk2[a], nsa_wv1[a], nsa_wv2[a])
            w_out = nsa_w_out[a]
        y_mem = _mem_attn(proj, kv, xq_blk, xg_blk, MEM_TQ)
        h = _out_proj(y_seq, y_mem, w_out, ln_post[i], h, OUT_TM)
    return h.reshape(batch, t, d)
```

```python
import functools
import itertools

import numpy as np
import jax
import jax.numpy as jnp
from jax import lax
from jax.experimental import pallas as pl
from jax.experimental.pallas import tpu as pltpu

F32 = jnp.float32
BF16 = jnp.bfloat16

D_MODEL = 2048
HEAD_DIM = 128
EPS = 1e-6
X_HEADS = 4
X_WIDTH = X_HEADS * HEAD_DIM
BRANCH_WIDTH = D_MODEL - X_WIDTH
GLA_HEADS = 4
GLA_DV = BRANCH_WIDTH // GLA_HEADS
GLA_DK = GLA_DV // 2
GLA_DKP = 256
GLA_RANK = 16
GLA_RANKP = 128
GLA_TEMP = 16.0
GLA_CHUNK = 64
GLA_SUB = 16
NSA_HEADS = BRANCH_WIDTH // HEAD_DIM
NSA_GROUPS = 2
NSA_HPG = NSA_HEADS // NSA_GROUPS
NSA_GW = NSA_HPG * HEAD_DIM
N_BRANCH = 3
CMP_LEN = 32
CMP_STRIDE = 16
CMP_HIDDEN = 256
SLC_LEN = 64
SLC_SHIFT = SLC_LEN.bit_length() - 1
SLC_TOPN = 16
WINDOW = 512
FORCED = 1e4
NEG = -1e30
PICKED = -3e38
ATT_SCALE = HEAD_DIM ** -0.5
LOG2E = 1.4426950408889634

LANES = 128
VMEM_LIMIT = 56 * 1024 * 1024

PROJ_TM = 1024
GLA_TN, NSA_TN = 1280, 1792
GLA_TB = 512
GLA_HEADS_PER_STEP = 4
MEM_TQ = 512
OUT_TM = 256
CMP_MLP_TR = 512
CMP_TQ = 512
TOPK_TR = 1024
SLC_TILE = 1024
SLC_ROWS = 512
CMP_ROWS = 128
WIN_ROWS = 256
TOPK_LANES = 256

GLA_V, GLA_R, GLA_Q, GLA_K, GLA_XQ, GLA_XG, GLA_GLR, GLA_NP = 0, 1536, 3072, 4096, 5120, 5632, 6144, 6400
(NSA_Q, NSA_R, NSA_KC, NSA_VC, NSA_VS, NSA_VW, NSA_XQ, NSA_XG, NSA_GL, NSA_NP) = (
    0, 1536, 3072, 3328, 3584, 3840, 4096, 4608, 5120, 5376)
NSA_KT_SLC, NSA_KT_WIN = 0, NSA_GROUPS


def _params(sem):
    return pltpu.CompilerParams(dimension_semantics=sem, vmem_limit_bytes=VMEM_LIMIT)


def _silu(x):
    return x * jax.nn.sigmoid(x)


def _dot_nt(a, b):
    return lax.dot_general(a, b, (((1,), (1,)), ((), ())), preferred_element_type=F32)


def _dot_tn(a, b):
    return lax.dot_general(a, b, (((0,), (0,)), ((), ())), preferred_element_type=F32)


def _rep(x, width):
    return x[:, :width] if width < LANES else jnp.concatenate([x] * (width // LANES), axis=1)


def _split_dot_nt(w, x):
    hi = x.astype(BF16)
    lo = (x - hi.astype(F32)).astype(BF16)
    return _dot_nt(w, hi) + _dot_nt(w, lo)


def _split_dot_left(w, x):
    hi = x.astype(BF16)
    lo = (x - hi.astype(F32)).astype(BF16)
    return (jnp.dot(w, hi, preferred_element_type=F32) + jnp.dot(w, lo, preferred_element_type=F32))


def _norm_proj_body(x_ref, g_ref, w_ref, *rest, transposed):
    if transposed:
        wt_ref, o_ref, ot_ref, xn_ref = rest
    else:
        o_ref, xn_ref = rest

    @pl.when(pl.program_id(1) == 0)
    def _():
        x = x_ref[...]
        r = lax.rsqrt(jnp.mean(x * x, axis=-1, keepdims=True) + EPS)
        xn_ref[...] = ((x * r) * g_ref[...]).astype(BF16)
        if transposed:
            ot_ref[...] = _dot_nt(wt_ref[...], xn_ref[...]).astype(ot_ref.dtype)

    o_ref[...] = jnp.dot(xn_ref[...], w_ref[...], preferred_element_type=F32).astype(o_ref.dtype)


def _norm_proj(x, g, w, tm, tn, wt=None):
    rows, d = x.shape
    n = w.shape[1]
    in_specs = [pl.BlockSpec((tm, d), lambda i, j: (i, 0)),
                pl.BlockSpec((1, d), lambda i, j: (0, 0)),
                pl.BlockSpec((d, tn), lambda i, j: (0, j))]
    out_specs = pl.BlockSpec((tm, tn), lambda i, j: (i, j))
    out_shape = jax.ShapeDtypeStruct((rows, n), BF16)
    args = (x, g.reshape(1, d), w)
    if wt is not None:
        nt = wt.shape[0]
        in_specs.append(pl.BlockSpec((nt, d), lambda i, j: (0, 0)))
        out_specs = [out_specs, pl.BlockSpec((nt, tm), lambda i, j: (0, i))]
        out_shape = [out_shape, jax.ShapeDtypeStruct((nt, rows), BF16)]
        args = args + (wt,)
    return pl.pallas_call(
        functools.partial(_norm_proj_body, transposed=wt is not None),
        grid=(rows // tm, n // tn),
        in_specs=in_specs,
        out_specs=out_specs,
        out_shape=out_shape,
        scratch_shapes=[pltpu.VMEM((tm, d), BF16)],
        compiler_params=_params(("parallel", "arbitrary")),
        name="norm_proj",
    )(*args)


def _mem_attn_body(xq_ref, xg_ref, kv_ref, o_ref):
    for h in range(X_HEADS):
        c = slice(h * HEAD_DIM, (h + 1) * HEAD_DIM)
        k = kv_ref[:, c]
        v = kv_ref[:, X_WIDTH + h * HEAD_DIM: X_WIDTH + (h + 1) * HEAD_DIM]
        s = _dot_nt(xq_ref[:, c], k) * ATT_SCALE
        e = jnp.exp(s - jnp.max(s, axis=-1, keepdims=True))
        p = e / jnp.sum(e, axis=-1, keepdims=True)
        o = jnp.dot(p.astype(BF16), v, preferred_element_type=F32)
        o_ref[:, c] = (o * _silu(xg_ref[:, c].astype(F32))).astype(o_ref.dtype)


def _mem_attn(proj, kv, xq_blk, xg_blk, tq):
    t = proj.shape[0]
    m = kv.shape[0]
    return pl.pallas_call(
        _mem_attn_body,
        grid=(t // tq,),
        in_specs=[pl.BlockSpec((tq, X_WIDTH), lambda i: (i, xq_blk)),
                  pl.BlockSpec((tq, X_WIDTH), lambda i: (i, xg_blk)),
                  pl.BlockSpec((m, 2 * X_WIDTH), lambda i: (0, 0))],
        out_specs=pl.BlockSpec((tq, X_WIDTH), lambda i: (i, 0)),
        out_shape=jax.ShapeDtypeStruct((t, X_WIDTH), BF16),
        compiler_params=_params(("parallel",)),
        name="mem_attn",
    )(proj, proj, kv)


def _out_proj_body(ys_ref, ym_ref, ws_ref, wm_ref, g_ref, h_ref, o_ref):
    y = (jnp.dot(ys_ref[...], ws_ref[...], preferred_element_type=F32)
         + jnp.dot(ym_ref[...], wm_ref[...], preferred_element_type=F32))
    r = lax.rsqrt(jnp.mean(y * y, axis=-1, keepdims=True) + EPS)
    o_ref[...] = h_ref[...] + (y * r) * g_ref[...]


def _out_proj(y_seq, y_mem, w_out, g_post, h, tm):
    t, d = h.shape
    ws = w_out[:BRANCH_WIDTH].astype(BF16)
    wm = w_out[BRANCH_WIDTH:].astype(BF16)
    return pl.pallas_call(
        _out_proj_body,
        grid=(t // tm,),
        in_specs=[pl.BlockSpec((tm, BRANCH_WIDTH), lambda i: (i, 0)),
                  pl.BlockSpec((tm, X_WIDTH), lambda i: (i, 0)),
                  pl.BlockSpec((BRANCH_WIDTH, d), lambda i: (0, 0)),
                  pl.BlockSpec((X_WIDTH, d), lambda i: (0, 0)),
                  pl.BlockSpec((1, d), lambda i: (0, 0)),
                  pl.BlockSpec((tm, d), lambda i: (i, 0))],
        out_specs=pl.BlockSpec((tm, d), lambda i: (i, 0)),
        out_shape=jax.ShapeDtypeStruct((t, d), F32),
        compiler_params=_params(("parallel",)),
        name="out_proj",
    )(y_seq, y_mem, ws, wm, g_post.reshape(1, d), h)


def _gla_body(q_ref, k_ref, v_ref, r_ref, glr_ref, wg_ref, bg_ref, gn_ref, tri_ref, y_ref, *scratch, chunks, heads):
    C = GLA_CHUNK
    SB = GLA_SUB
    per_head = len(scratch) // heads
    state = [scratch[hh * per_head] for hh in range(heads)]
    bufs = [[scratch[hh * per_head + 1 + 4 * par: hh * per_head + 5 + 4 * par] for par in range(2)]
            for hh in range(heads)]

    @pl.when(pl.program_id(1) == 0)
    def _():
        for st_ref in state:
            st_ref[...] = jnp.zeros_like(st_ref)

    def free_part(hh, c, par):
        rows = pl.ds(pl.multiple_of(c * C, C), C)
        kc = slice(hh * GLA_DKP, (hh + 1) * GLA_DKP)
        vc = slice(hh * GLA_DV, (hh + 1) * GLA_DV)
        b_h, q_h, k_h, o_h = bufs[hh][par]
        gp = jnp.dot(glr_ref[rows, :], wg_ref[hh], preferred_element_type=F32) + bg_ref[hh]
        logg = (jnp.minimum(gp, 0.0) - jnp.log1p(jnp.exp(-jnp.abs(gp)))) * (LOG2E / GLA_TEMP)
        b_h[...] = _split_dot_left(tri_ref[...], logg)
        q_h[...] = q_ref[rows, kc].astype(F32) * (GLA_DK ** -0.5)
        k_h[...] = k_ref[rows, kc].astype(F32)
        yield

        qs, ks = [], []
        for j in range(C // SB - 1):
            lo, hi = j * SB, (j + 1) * SB
            beta = b_h[hi:hi + 1, :]
            qt = (q_h[hi:, :] * jnp.exp2(b_h[hi:, :] - beta)).astype(BF16)
            kt = (k_h[lo:hi, :] * jnp.exp2(beta - b_h[lo:hi, :])).astype(BF16)
            qs.append(jnp.concatenate([jnp.zeros((hi, GLA_DKP), BF16), qt], axis=0))
            ks.append(jnp.concatenate(([jnp.zeros((lo, GLA_DKP), BF16)] if lo else []) + [kt]
                                      + [jnp.zeros((C - hi, GLA_DKP), BF16)], axis=0))
        coef_between = _dot_nt(jnp.concatenate(qs, axis=1), jnp.concatenate(ks, axis=1))
        yield

        sub8 = lax.broadcasted_iota(jnp.int32, (8, 1), 0)
        lane = lax.broadcasted_iota(jnp.int32, (1, LANES), 1)
        tiles = []
        for blk in range(C // SB):
            base = blk * SB
            coef = [jnp.zeros((8, LANES), F32) for _ in range(SB // 8)]
            for s in range(SB):
                row = base + s
                bs = b_h[row:row + 1, :]
                ks_row = k_h[row:row + 1, :]
                for tix in range(s // 8, SB // 8):
                    r0 = base + tix * 8
                    e = jnp.exp2(b_h[r0:r0 + 8, :] - bs)
                    if tix == s // 8:
                        e = jnp.where(sub8 >= s % 8, e, 0.0)
                    a = jnp.sum(q_h[r0:r0 + 8, :] * ks_row * e, axis=-1, keepdims=True)
                    coef[tix] = jnp.where(lane == row, a, coef[tix])
            tiles += coef
            yield
        coef_all = (coef_between + jnp.concatenate(tiles, axis=0)[:, :C]).astype(BF16)
        o_h[...] = jnp.dot(coef_all, v_ref[rows, vc], preferred_element_type=F32)

    def state_part(hh, c, par):
        rows = pl.ds(pl.multiple_of(c * C, C), C)
        vc = slice(hh * GLA_DV, (hh + 1) * GLA_DV)
        b_h, q_h, k_h, o_h = bufs[hh][par]
        st_ref = state[hh]
        b = b_h[...]
        st = st_ref[...]
        o = o_h[...] + _dot_nt((q_h[...] * jnp.exp2(b)).astype(BF16), st.astype(BF16))
        yield
        bl = b[C - 1:C, :]
        kd = (k_h[...] * jnp.exp2(bl - b)).astype(BF16)
        st_ref[...] = jnp.exp2(bl) * st + _dot_tn(v_ref[rows, vc], kd)
        yield
        on = (o * lax.rsqrt(jnp.mean(o * o, axis=-1, keepdims=True) + EPS)) * gn_ref[...]
        y_ref[rows, vc] = (on * _silu(r_ref[rows, vc].astype(F32))).astype(y_ref.dtype)

    def run(*gens):
        for _ in itertools.zip_longest(*gens):
            pass

    def pair(c, with_next):
        run(*[state_part(hh, c, 0) for hh in range(heads)], *[free_part(hh, c + 1, 1) for hh in range(heads)])
        run(*[state_part(hh, c + 1, 1) for hh in range(heads)],
            *([free_part(hh, c + 2, 0) for hh in range(heads)] if with_next else []))

    run(*[free_part(hh, 0, 0) for hh in range(heads)])

    def body(c2, carry):
        pair(2 * c2, True)
        return carry

    lax.fori_loop(0, chunks // 2 - 1, body, 0)
    pair(chunks - 2, False)


def _gla_scan(proj, wg, bg, gnorm, tb, heads):
    t = proj.shape[0]
    C = GLA_CHUNK
    tri = jnp.asarray(np.tril(np.ones((C, C), np.float32)), BF16)
    kw, vw = heads * GLA_DKP, heads * GLA_DV
    qb, kb, vb, rb, gb = GLA_Q // kw, GLA_K // kw, GLA_V // vw, GLA_R // vw, GLA_GLR // GLA_RANKP
    return pl.pallas_call(
        functools.partial(_gla_body, chunks=tb // C, heads=heads),
        grid=(GLA_HEADS // heads, t // tb),
        in_specs=[pl.BlockSpec((tb, kw), lambda h, n: (n, qb + h)),
                  pl.BlockSpec((tb, kw), lambda h, n: (n, kb + h)),
                  pl.BlockSpec((tb, vw), lambda h, n: (n, vb + h)),
                  pl.BlockSpec((tb, vw), lambda h, n: (n, rb + h)),
                  pl.BlockSpec((tb, GLA_RANKP), lambda h, n: (n, gb)),
                  pl.BlockSpec((heads, GLA_RANKP, GLA_DKP), lambda h, n: (h, 0, 0)),
                  pl.BlockSpec((heads, 1, GLA_DKP), lambda h, n: (h, 0, 0)),
                  pl.BlockSpec((1, GLA_DV), lambda h, n: (0, 0)),
                  pl.BlockSpec((C, C), lambda h, n: (0, 0))],
        out_specs=pl.BlockSpec((tb, vw), lambda h, n: (n, h)),
        out_shape=jax.ShapeDtypeStruct((t, BRANCH_WIDTH), BF16),
        scratch_shapes=([pltpu.VMEM((GLA_DV, GLA_DKP), F32)]
                        + [pltpu.VMEM((C, GLA_DKP), F32), pltpu.VMEM((C, GLA_DKP), F32),
                           pltpu.VMEM((C, GLA_DKP), F32), pltpu.VMEM((C, GLA_DV), F32)] * 2) * heads,
        compiler_params=_params(("parallel", "arbitrary")),
        name="gla_scan",
    )(proj, proj, proj, proj, proj, wg, bg, gnorm.reshape(1, GLA_DV), tri)


def _pad_heads(w, heads, width, padded):
    d = w.shape[0]
    return jnp.pad(w.reshape(d, heads, width), ((0, 0), (0, 0), (0, padded - width))).reshape(d, heads * padded)


def _gla_layer_seq(xn_proj_fn, w_in, w_gate_lr, b_gate, g_norm):
    sizes = np.cumsum([GLA_HEADS * GLA_DK, GLA_HEADS * GLA_DK, BRANCH_WIDTH, GLA_RANK, BRANCH_WIDTH, X_WIDTH])
    wq, wk, wv, wglr, wr, wxq, wxg = jnp.split(w_in, sizes.tolist(), axis=1)
    w_cat = jnp.concatenate([
        wv, wr, _pad_heads(wq, GLA_HEADS, GLA_DK, GLA_DKP), _pad_heads(wk, GLA_HEADS, GLA_DK, GLA_DKP), wxq, wxg,
        jnp.pad(wglr, ((0, 0), (0, GLA_NP - GLA_GLR - GLA_RANK)))], axis=1).astype(BF16)
    proj = xn_proj_fn(w_cat, GLA_TN)
    wg = jnp.pad(w_gate_lr.reshape(GLA_RANK, GLA_HEADS, GLA_DK).transpose(1, 0, 2),
                 ((0, 0), (0, GLA_RANKP - GLA_RANK), (0, GLA_DKP - GLA_DK))).astype(BF16)
    bg = jnp.pad(b_gate.reshape(GLA_HEADS, 1, GLA_DK), ((0, 0), (0, 0), (0, GLA_DKP - GLA_DK)))
    y_seq = _gla_scan(proj, wg, bg, g_norm, GLA_TB, GLA_HEADS_PER_STEP)
    return proj, y_seq, GLA_XQ // X_WIDTH, GLA_XG // X_WIDTH


def _cmp_mlp_body(x_ref, pe_ref, w1_ref, w2_ref, o_ref):
    x = (x_ref[...].astype(F32) + pe_ref[...]).astype(BF16)
    hid = _silu(jnp.dot(x, w1_ref[...], preferred_element_type=F32))
    o_ref[...] = jnp.dot(hid.astype(BF16), w2_ref[...], preferred_element_type=F32).astype(o_ref.dtype)


def _cmp_mlp(blocks, pe, w1, w2, tr):
    _, rows, width = blocks.shape
    return pl.pallas_call(
        _cmp_mlp_body,
        grid=(2, rows // tr),
        in_specs=[pl.BlockSpec((None, tr, width), lambda a, i: (a, i, 0)),
                  pl.BlockSpec((None, 1, width), lambda a, i: (a, 0, 0)),
                  pl.BlockSpec((None, width, CMP_HIDDEN), lambda a, i: (a, 0, 0)),
                  pl.BlockSpec((None, CMP_HIDDEN, HEAD_DIM), lambda a, i: (a, 0, 0))],
        out_specs=pl.BlockSpec((None, tr, HEAD_DIM), lambda a, i: (a, i, 0)),
        out_shape=jax.ShapeDtypeStruct((2, rows, HEAD_DIM), BF16),
        compiler_params=_params(("parallel", "parallel")),
        name="cmp_mlp",
    )(blocks, pe, w1, w2)


def _cmp_attn_body(q_ref, kct_ref, vca_ref, ovt_ref, o_ref, imp_ref, qs_sc, imp_sc, *, tq, widths):
    i = pl.program_id(0)
    n_tiles = pl.num_programs(0)
    for hh in range(NSA_HEADS):
        c = slice(hh * HEAD_DIM, (hh + 1) * HEAD_DIM)
        qs_sc[hh] = (q_ref[:, c].astype(F32) * (ATT_SCALE * LOG2E)).astype(BF16)

    def run(w):
        t = i * tq + lax.broadcasted_iota(jnp.int32, (tq, 1), 0)
        n = lax.broadcasted_iota(jnp.int32, (1, w), 1)
        bias = jnp.where(n * CMP_STRIDE + (CMP_LEN - 1) <= t, 0.0, NEG)
        has_key = t >= CMP_LEN - 1
        R = CMP_ROWS
        for g in range(NSA_GROUPS):
            for h in range(NSA_HPG):
                hh = g * NSA_HPG + h
                c = slice(hh * HEAD_DIM, (hh + 1) * HEAD_DIM)
                for r in range(tq // R):
                    rows = slice(r * R, (r + 1) * R)
                    s = jnp.dot(qs_sc[hh, rows, :], kct_ref[g, :, :w], preferred_element_type=F32) + bias[rows, :]
                    e = jnp.exp2(s - jnp.max(s, axis=-1, keepdims=True))
                    pv = jnp.dot(e.astype(BF16), vca_ref[g, :w, :], preferred_element_type=F32)
                    inv = jnp.where(has_key[rows, :], 1.0 / pv[:, HEAD_DIM:], 0.0)
                    o_ref[rows, c] = (pv[:, :HEAD_DIM] * inv).astype(o_ref.dtype)
                    p = e * _rep(inv, w)
                    if h == 0:
                        imp_sc[rows, :w] = p
                    else:
                        imp_sc[rows, :w] += p
            imp_ref[g] = _split_dot_nt(ovt_ref[:, :w], imp_sc[:, :w])

    per_class = n_tiles // len(widths)
    for cls, w in enumerate(widths):
        pl.when(i // per_class == cls)(functools.partial(run, w))


def _cmp_attn(proj, kct, vca, overlap_t, tq):
    t = proj.shape[0]
    nb, nc = overlap_t.shape
    classes = 4
    widths = tuple(nc * (c + 1) // classes for c in range(classes))
    assert (t // tq) % classes == 0 and all(w % min(LANES, nc // classes) == 0 for w in widths)
    return pl.pallas_call(
        functools.partial(_cmp_attn_body, tq=tq, widths=widths),
        grid=(t // tq,),
        in_specs=[pl.BlockSpec((tq, BRANCH_WIDTH), lambda i: (i, NSA_Q // BRANCH_WIDTH)),
                  pl.BlockSpec((NSA_GROUPS, HEAD_DIM, nc), lambda i: (0, 0, 0)),
                  pl.BlockSpec((NSA_GROUPS, nc, 2 * HEAD_DIM), lambda i: (0, 0, 0)),
                  pl.BlockSpec((nb, nc), lambda i: (0, 0))],
        out_specs=[pl.BlockSpec((tq, BRANCH_WIDTH), lambda i: (i, 0)),
                   pl.BlockSpec((NSA_GROUPS, nb, tq), lambda i: (0, 0, i))],
        out_shape=[jax.ShapeDtypeStruct((t, BRANCH_WIDTH), BF16),
                   jax.ShapeDtypeStruct((NSA_GROUPS, nb, t), F32)],
        scratch_shapes=[pltpu.VMEM((NSA_HEADS, tq, HEAD_DIM), BF16),
                        pltpu.VMEM((tq, nc), F32)],
        compiler_params=_params(("parallel",)),
        name="cmp_attn",
    )(proj, kct, vca, overlap_t)


def _topk_body(imp_ref, sel_ref, x_sc, *, tr, nb):
    t = pl.program_id(1) * tr + lax.broadcasted_iota(jnp.int32, (1, tr), 1)
    jb = lax.broadcasted_iota(jnp.int32, (nb, 1), 0)
    cur = jnp.right_shift(t, SLC_SHIFT)
    x = imp_ref[...]
    x = jnp.where((jb == 0) | (jb == cur) | (jb == cur - 1), FORCED, x)
    x_sc[...] = jnp.where(jb > cur, -FORCED, x)
    R = min(TOPK_LANES, tr)
    jbf = lax.broadcasted_iota(jnp.int32, (nb, R), 0).astype(F32)

    def pick(_, carry):
        for r in range(tr // R):
            cols = slice(r * R, (r + 1) * R)
            x = x_sc[:, cols]
            m = jnp.max(x, axis=0, keepdims=True)
            idx = jnp.min(jnp.where(x == m, jbf, float(nb)), axis=0, keepdims=True)
            x_sc[:, cols] = jnp.where(jbf == idx, PICKED, x)
        return carry

    lax.fori_loop(0, SLC_TOPN, pick, 0)
    sel_ref[...] = jnp.where(x_sc[...] < 0.5 * PICKED, 0.0, NEG).astype(sel_ref.dtype)


def _topk_select(imp_t, tr):
    groups, nb, t = imp_t.shape
    return pl.pallas_call(
        functools.partial(_topk_body, tr=tr, nb=nb),
        grid=(groups, t // tr),
        in_specs=[pl.BlockSpec((None, nb, tr), lambda g, i: (g, 0, i))],
        out_specs=pl.BlockSpec((None, nb, tr), lambda g, i: (g, 0, i)),
        out_shape=jax.ShapeDtypeStruct(imp_t.shape, BF16),
        scratch_shapes=[pltpu.VMEM((nb, tr), F32)],
        compiler_params=_params(("parallel", "parallel")),
        name="topk_select",
    )(imp_t)


def _win_attn_body(q_ref, kp_ref, kc_ref, vp_ref, vc_ref, o_ref, qs_sc, vpa_sc, vca_sc, *, tq):
    i = pl.program_id(1)
    R = WIN_ROWS
    for h in range(NSA_HPG):
        qs_sc[h] = (q_ref[:, h * HEAD_DIM:(h + 1) * HEAD_DIM].astype(F32) * (ATT_SCALE * LOG2E)).astype(BF16)
    for src, dst in ((vp_ref, vpa_sc), (vc_ref, vca_sc)):
        dst[:, :HEAD_DIM] = src[...]
        dst[:, HEAD_DIM:] = jnp.ones((tq, HEAD_DIM), BF16)
    col = lax.broadcasted_iota(jnp.int32, (1, R), 1)
    row = lax.broadcasted_iota(jnp.int32, (R, 1), 0)
    upto_query = jnp.where(col <= row, 0.0, NEG)
    inside_window = jnp.where(col > row, 0.0, NEG)

    def run(with_prev):
        for h in range(NSA_HPG):
            for r in range(tq // R):
                rows = slice(r * R, (r + 1) * R)
                n_cur = (r + 1) * R
                sc = jnp.dot(qs_sc[h, rows, :], kc_ref[:, :n_cur], preferred_element_type=F32)
                parts = ([sc[:, :n_cur - R]] if r > 0 else []) + [sc[:, n_cur - R:] + upto_query]
                if with_prev:
                    n_prev = tq - r * R
                    sp = jnp.dot(qs_sc[h, rows, :], kp_ref[:, r * R:], preferred_element_type=F32)
                    parts = [sp[:, :R] + inside_window] + ([sp[:, R:]] if n_prev > R else []) + parts
                s = parts[0] if len(parts) == 1 else jnp.concatenate(parts, axis=1)
                e = jnp.exp2(s - jnp.max(s, axis=-1, keepdims=True)).astype(BF16)
                if with_prev:
                    pv = (jnp.dot(e[:, :n_prev], vpa_sc[r * R:, :], preferred_element_type=F32)
                          + jnp.dot(e[:, n_prev:], vca_sc[:n_cur, :], preferred_element_type=F32))
                else:
                    pv = jnp.dot(e, vca_sc[:n_cur, :], preferred_element_type=F32)
                o_ref[rows, h * HEAD_DIM:(h + 1) * HEAD_DIM] = (pv[:, :HEAD_DIM] / pv[:, HEAD_DIM:]).astype(o_ref.dtype)

    pl.when(i == 0)(functools.partial(run, False))
    pl.when(i > 0)(functools.partial(run, True))


def _win_attn(proj, kt, tq):
    t = proj.shape[0]
    assert tq == WINDOW
    prev = lambda i: jnp.maximum(i - 1, 0)
    vb = NSA_VW // HEAD_DIM
    return pl.pallas_call(
        functools.partial(_win_attn_body, tq=tq),
        grid=(NSA_GROUPS, t // tq),
        in_specs=[pl.BlockSpec((tq, NSA_GW), lambda g, i: (i, g)),
                  pl.BlockSpec((HEAD_DIM, tq), lambda g, i: (NSA_KT_WIN + g, prev(i))),
                  pl.BlockSpec((HEAD_DIM, tq), lambda g, i: (NSA_KT_WIN + g, i)),
                  pl.BlockSpec((tq, HEAD_DIM), lambda g, i: (prev(i), vb + g)),
                  pl.BlockSpec((tq, HEAD_DIM), lambda g, i: (i, vb + g))],
        out_specs=pl.BlockSpec((tq, NSA_GW), lambda g, i: (i, g)),
        out_shape=jax.ShapeDtypeStruct((t, BRANCH_WIDTH), BF16),
        scratch_shapes=[pltpu.VMEM((NSA_HPG, tq, HEAD_DIM), BF16),
                        pltpu.VMEM((tq, 2 * HEAD_DIM), BF16),
                        pltpu.VMEM((tq, 2 * HEAD_DIM), BF16)],
        compiler_params=_params(("parallel", "parallel")),
        name="win_attn",
    )(proj, kt, kt, proj, proj)


def _slc_attn_body(it_ref, jt_ref, q_ref, k_ref, v_ref, sel_ref, ocmp_ref, owin_ref, gl_ref, r_ref, y_ref,
                   qa_sc, m_sc, acc_sc, ka_sc, va_sc, *, tq, tk, lw):
    step = pl.program_id(1)
    i = it_ref[step]
    j = jt_ref[step]

    @pl.when(j == 0)
    def _():
        for h in range(NSA_HPG):
            cs = slice(h * HEAD_DIM, (h + 1) * HEAD_DIM)
            qa_sc[h, :, :HEAD_DIM] = (q_ref[:, cs].astype(F32) * (ATT_SCALE * LOG2E)).astype(BF16)
        m_sc[...] = jnp.full_like(m_sc, NEG)
        acc_sc[...] = jnp.zeros_like(acc_sc)

    sel = sel_ref[...]
    for h in range(NSA_HPG):
        qa_sc[h, :, HEAD_DIM:] = sel
    key_blk = (j * (tk // SLC_LEN) + jnp.right_shift(lax.broadcasted_iota(jnp.int32, (1, tk), 1), SLC_SHIFT)) & (lw - 1)
    ka_sc[:HEAD_DIM, :] = k_ref[...]
    ka_sc[HEAD_DIM:, :] = jnp.where(lax.broadcasted_iota(jnp.int32, (lw, 1), 0) == key_blk, 1.0, 0.0).astype(BF16)
    va_sc[:, :HEAD_DIM] = v_ref[...]
    va_sc[:, HEAD_DIM:] = jnp.ones((tk, HEAD_DIM), BF16)

    def accumulate(diag):
        R = SLC_ROWS
        if diag:
            tri = jnp.where(lax.broadcasted_iota(jnp.int32, (1, R), 1) <= lax.broadcasted_iota(jnp.int32, (R, 1), 0),
                            0.0, NEG)
        for h in range(NSA_HPG):
            for r in range(tq // R):
                rows = slice(r * R, (r + 1) * R)
                ncols = (r + 1) * R if diag else tk
                s = jnp.dot(qa_sc[h, rows, :], ka_sc[:, :ncols], preferred_element_type=F32)
                if diag:
                    tail = s[:, ncols - R:] + tri
                    s = tail if r == 0 else jnp.concatenate([s[:, :ncols - R], tail], axis=1)
                m_old = m_sc[h, rows, :]
                m_new = jnp.maximum(m_old, jnp.max(s, axis=-1, keepdims=True))
                p = jnp.exp2(s - _rep(m_new, ncols))
                alpha = jnp.exp2(m_old - m_new)
                acc_sc[h, rows, :] = (_rep(alpha, 2 * HEAD_DIM) * acc_sc[h, rows, :]
                                      + jnp.dot(p.astype(BF16), va_sc[:ncols, :], preferred_element_type=F32))
                m_sc[h, rows, :] = m_new

    @pl.when(j < i)
    def _():
        accumulate(False)

    @pl.when(j == i)
    def _():
        accumulate(True)
        gates = jax.nn.sigmoid(gl_ref[...].astype(F32))
        for h in range(NSA_HPG):
            cs = slice(h * HEAD_DIM, (h + 1) * HEAD_DIM)
            o_slc = acc_sc[h, :, :HEAD_DIM] / acc_sc[h, :, HEAD_DIM:]
            g0 = gates[:, h * N_BRANCH + 0:h * N_BRANCH + 1]
            g1 = gates[:, h * N_BRANCH + 1:h * N_BRANCH + 2]
            g2 = gates[:, h * N_BRANCH + 2:h * N_BRANCH + 3]
            o = g0 * ocmp_ref[:, cs].astype(F32) + g1 * o_slc + g2 * owin_ref[:, cs].astype(F32)
            y_ref[:, cs] = (o * _silu(r_ref[:, cs].astype(F32))).astype(y_ref.dtype)


def _slc_attn(proj, kt, sel, o_cmp, o_win, tile):
    t = proj.shape[0]
    nb = sel.shape[1] // NSA_GROUPS
    lw = min(LANES, nb)
    assert lw & (lw - 1) == 0 and (tile // SLC_LEN) <= lw
    ni = t // tile
    it = np.concatenate([np.full(i + 1, i) for i in range(ni)]).astype(np.int32)
    jt = np.concatenate([np.arange(i + 1) for i in range(ni)]).astype(np.int32)
    blocks_per_tile = tile // SLC_LEN
    sel_idx = lambda g, s, it, jt: (it[s], g * (nb // lw) + (jt[s] * blocks_per_tile) // lw)
    row = lambda g, s, it, jt: (it[s], g)
    grid_spec = pltpu.PrefetchScalarGridSpec(
        num_scalar_prefetch=2,
        grid=(NSA_GROUPS, len(it)),
        in_specs=[pl.BlockSpec((tile, NSA_GW), row),
                  pl.BlockSpec((HEAD_DIM, tile), lambda g, s, it, jt: (NSA_KT_SLC + g, jt[s])),
                  pl.BlockSpec((tile, HEAD_DIM), lambda g, s, it, jt: (jt[s], NSA_VS // HEAD_DIM + g)),
                  pl.BlockSpec((tile, lw), sel_idx),
                  pl.BlockSpec((tile, NSA_GW), row),
                  pl.BlockSpec((tile, NSA_GW), row),
                  pl.BlockSpec((tile, HEAD_DIM), lambda g, s, it, jt: (it[s], NSA_GL // HEAD_DIM + g)),
                  pl.BlockSpec((tile, NSA_GW), lambda g, s, it, jt: (it[s], NSA_R // NSA_GW + g))],
        out_specs=pl.BlockSpec((tile, NSA_GW), row),
        scratch_shapes=[pltpu.VMEM((NSA_HPG, tile, HEAD_DIM + lw), BF16),
                        pltpu.VMEM((NSA_HPG, tile, HEAD_DIM), F32),
                        pltpu.VMEM((NSA_HPG, tile, 2 * HEAD_DIM), F32),
                        pltpu.VMEM((HEAD_DIM + lw, tile), BF16),
                        pltpu.VMEM((tile, 2 * HEAD_DIM), BF16)])
    return pl.pallas_call(
        functools.partial(_slc_attn_body, tq=tile, tk=tile, lw=lw),
        grid_spec=grid_spec,
        out_shape=jax.ShapeDtypeStruct((t, BRANCH_WIDTH), BF16),
        compiler_params=_params(("parallel", "arbitrary")),
        name="slc_attn",
    )(jnp.asarray(it), jnp.asarray(jt), proj, kt, proj, sel, o_cmp, o_win, proj, proj)


def _nsa_layer_seq(xn_proj_fn, w_in, pe_k, pe_v, wk1, wk2, wv1, wv2):
    t_sizes = [BRANCH_WIDTH] + [NSA_GROUPS * HEAD_DIM] * 6 + [NSA_HEADS * N_BRANCH, BRANCH_WIDTH, X_WIDTH]
    wq, wkc, wvc, wks, wvs, wkw, wvw, wgl, wr, wxq, wxg = jnp.split(w_in, np.cumsum(t_sizes).tolist(), axis=1)
    wgl = _pad_heads(wgl, NSA_GROUPS, NSA_HPG * N_BRANCH, HEAD_DIM)
    w_cat = jnp.concatenate([wq, wr, wkc, wvc, wvs, wvw, wxq, wxg, wgl], axis=1).astype(BF16)
    wt = jnp.concatenate([wks, wkw], axis=1).T.astype(BF16)
    proj, kt = xn_proj_fn(w_cat, NSA_TN, wt)
    t = proj.shape[0]

    half = CMP_STRIDE
    nc = t // half
    a = proj[:, NSA_KC:NSA_VS].reshape(nc, half, 2, NSA_GROUPS, HEAD_DIM)
    a = a.transpose(2, 3, 0, 1, 4).reshape(2, NSA_GROUPS, nc, half * HEAD_DIM)
    blocks = jnp.concatenate([a, jnp.roll(a, -1, axis=2)], axis=-1).reshape(2, NSA_GROUPS * nc, CMP_LEN * HEAD_DIM)
    pe = jnp.stack([pe_k, pe_v]).reshape(2, 1, CMP_LEN * HEAD_DIM)
    kv_cmp = _cmp_mlp(blocks, pe, jnp.stack([wk1, wv1]).astype(BF16), jnp.stack([wk2, wv2]).astype(BF16),
                      min(CMP_MLP_TR, NSA_GROUPS * nc))
    kv_cmp = kv_cmp.reshape(2, NSA_GROUPS, nc, HEAD_DIM)

    nb = t // SLC_LEN
    cstart = np.arange(nc) * CMP_STRIDE
    sstart = np.arange(nb) * SLC_LEN
    overlap = ((cstart[:, None] < sstart[None, :] + SLC_LEN) & (cstart[:, None] + CMP_LEN > sstart[None, :]))
    overlap[nc - 1] = False
    kct = kv_cmp[0].transpose(0, 2, 1)
    vca = jnp.concatenate([kv_cmp[1], jnp.ones_like(kv_cmp[1])], axis=-1)
    o_cmp, imp_t = _cmp_attn(proj, kct, vca, jnp.asarray(overlap.T.astype(np.float32), BF16), CMP_TQ)
    sel = _topk_select(imp_t, min(TOPK_TR, t)).transpose(2, 0, 1).reshape(t, NSA_GROUPS * nb)
    o_win = _win_attn(proj, kt, WINDOW)
    y_seq = _slc_attn(proj, kt, sel, o_cmp, o_win, SLC_TILE)
    return proj, y_seq, NSA_XQ // X_WIDTH, NSA_XG // X_WIDTH


def kernel(x, mem, ln_pre, ln_post, ln_mem, w_mem_kv, gla_w_in, gla_w_gate_lr, gla_b_gate, gla_norm, gla_w_out,
           nsa_w_in, nsa_pe_k, nsa_pe_v, nsa_wk1, nsa_wk2, nsa_wv1, nsa_wv2, nsa_w_out):
    batch, t, d = x.shape
    assert batch == 1 and d == D_MODEL and t % PROJ_TM == 0
    h = x.reshape(t, d)
    mem2 = mem.reshape(mem.shape[1], d)
    depth = ln_pre.shape[0]
    for i in range(depth):
        a = i // 2
        xn_proj_fn = lambda w, tn, wt=None, i=i, h=h: _norm_proj(h, ln_pre[i], w, PROJ_TM, tn, wt)
        kv = _norm_proj(mem2, ln_mem[i], w_mem_kv[i].astype(BF16), mem2.shape[0], 2 * X_WIDTH)
        if i % 2 == 0:
            proj, y_seq, xq_blk, xg_blk = _gla_layer_seq(xn_proj_fn, gla_w_in[a], gla_w_gate_lr[a], gla_b_gate[a],
                                                         gla_norm[a])
            w_out = gla_w_out[a]
        else:
            proj, y_seq, xq_blk, xg_blk = _nsa_layer_seq(xn_proj_fn, nsa_w_in[a], nsa_pe_k[a], nsa_pe_v[a],
                                                         nsa_wk1[a], nsa_wk2[a], nsa_wv1[a], nsa_wv2[a])
            w_out = nsa_w_out[a]
        y_mem = _mem_attn(proj, kv, xq_blk, xg_blk, MEM_TQ)
        h = _out_proj(y_seq, y_mem, w_out, ln_post[i], h, OUT_TM)
    return h.reshape(batch, t, d)
```
